```python
import math
import jax, jax.numpy as jnp
from jax import lax
import numpy as np

D_MODEL = 1024
BATCH = 2
SEQ = 8192
DEPTH = 4
DEC_BATCH = 8
DEC_SEQ = 8192
PAST_LEN = 128

N_MEM = 256
HEAD_DIM = 64
D_HYENA = 512
D_CONF = 256
D_XATTN = 256
N_XHEADS = D_XATTN // HEAD_DIM
D_MIX = D_HYENA + D_CONF + D_XATTN
SHORT_K = 3
CONF_K = 31
FILT_BANDS = 16
FILT_EMB = 1 + 2 * FILT_BANDS
FILT_HIDDEN = 64
DECAY_TARGET = 1e-2
FAST_DECAY_PCT = 0.3
SLOW_DECAY_PCT = 1.5
EPS = 1e-6

P_HY_U = 3 * D_HYENA
P_HY_G = D_HYENA
P_CF_GLU = 2 * D_CONF
P_CF_G = D_CONF
P_XA_Q = D_XATTN
P_XA_G = D_XATTN
D_IN = P_HY_U + P_HY_G + P_CF_GLU + P_CF_G + P_XA_Q + P_XA_G
SPLITS = (P_HY_U,
          P_HY_U + P_HY_G,
          P_HY_U + P_HY_G + P_CF_GLU,
          P_HY_U + P_HY_G + P_CF_GLU + P_CF_G,
          P_HY_U + P_HY_G + P_CF_GLU + P_CF_G + P_XA_Q)

kernel_name = "hybrid_hyena_conformer_memattn_encoder"


def rmsnorm(x, g):
    xf = x.astype(jnp.float32)
    y = xf * lax.rsqrt(jnp.mean(xf * xf, axis=-1, keepdims=True) + EPS)
    return (y * g.astype(jnp.float32)).astype(x.dtype)


def layernorm(x, g, b):
    xf = x.astype(jnp.float32)
    mu = jnp.mean(xf, axis=-1, keepdims=True)
    xc = xf - mu
    y = xc * lax.rsqrt(jnp.mean(xc * xc, axis=-1, keepdims=True) + EPS)
    return (y * g.astype(jnp.float32) + b.astype(jnp.float32)).astype(x.dtype)


def depthwise_conv(x, w, b):
    k = w.shape[0]
    y = lax.conv_general_dilated(
        x, w[:, None, :].astype(x.dtype), window_strides=(1,),
        padding=[(k // 2, k // 2)], dimension_numbers=('NWC', 'WIO', 'NWC'),
        feature_group_count=x.shape[-1])
    return y + b.astype(x.dtype)


def hyena_positional(L):
    t = jnp.linspace(0.0, 1.0, L, dtype=jnp.float32)[:, None]
    n = jnp.arange(L, dtype=jnp.float32)[:, None]
    bands = jnp.linspace(1e-4, FILT_BANDS - 1, FILT_BANDS, dtype=jnp.float32)[None, :]
    ang = (2.0 * math.pi / L) * bands * n
    z = jnp.concatenate([t, jnp.cos(ang), -jnp.sin(ang)], axis=-1)
    return t, z


def hyena_two_sided_filter(L, w1, b1, fr1, w2, b2, fr2, w3, b3, fr3, w4):
    t, z = hyena_positional(L)
    f32 = jnp.float32
    h = jnp.sin(fr1.astype(f32) * (z @ w1.astype(f32) + b1.astype(f32)))
    h = jnp.sin(fr2.astype(f32) * (h @ w2.astype(f32) + b2.astype(f32)))
    h = jnp.sin(fr3.astype(f32) * (h @ w3.astype(f32) + b3.astype(f32)))
    h = h @ w4.astype(f32)
    max_decay = math.log(DECAY_TARGET) / FAST_DECAY_PCT
    min_decay = math.log(DECAY_TARGET) / SLOW_DECAY_PCT
    deltas = jnp.abs(jnp.linspace(min_decay, max_decay, D_HYENA, dtype=f32))
    deltas = jnp.concatenate([deltas, deltas])
    h = h * jnp.exp(-t * deltas)
    h_fwd, h_bwd = h[:, :D_HYENA], h[:, D_HYENA:]
    return jnp.concatenate([h_fwd, jnp.zeros((1, D_HYENA), f32), h_bwd[1:][::-1]], axis=0)


def bidir_fftconv(u, k2, skip):
    L = u.shape[1]
    uf32 = u.astype(jnp.float32)
    uf = jnp.fft.rfft(uf32, n=2 * L, axis=1)
    kf = jnp.fft.rfft(k2, n=2 * L, axis=0)
    y = jnp.fft.irfft(uf * kf[None], n=2 * L, axis=1)[:, :L]
    return (y + uf32 * skip.astype(jnp.float32)).astype(u.dtype)


def hyena_branch(u, short_w, short_b, k2, skip):
    u = depthwise_conv(u, short_w, short_b)
    x0, x1, v = jnp.split(u, 3, axis=-1)
    return x0 * bidir_fftconv(v * x1, k2, skip)


def conformer_branch(a, dw_w, dw_b, ln_g, ln_b):
    val, gate = jnp.split(a, 2, axis=-1)
    h = val * jax.nn.sigmoid(gate)
    h = depthwise_conv(h, dw_w, dw_b)
    h = layernorm(h, ln_g, ln_b)
    return jax.nn.silu(h)


def memory_attention(q, mem_n, w_kv):
    B, L, _ = q.shape
    M = mem_n.shape[1]
    k, v = jnp.split(mem_n @ w_kv, 2, axis=-1)
    q = q.reshape(B, L, N_XHEADS, HEAD_DIM)
    k = k.reshape(B, M, N_XHEADS, HEAD_DIM)
    v = v.reshape(B, M, N_XHEADS, HEAD_DIM)
    s = jnp.einsum('blhd,bmhd->bhlm', q, k).astype(jnp.float32) * (HEAD_DIM ** -0.5)
    p = jax.nn.softmax(s, axis=-1).astype(v.dtype)
    o = jnp.einsum('bhlm,bmhd->blhd', p, v)
    return o.reshape(B, L, D_XATTN)


def encoder_layer(x, mem, norm_g, mem_norm_g, w_in, hy_short_w, hy_short_b,
                  hy_f_w1, hy_f_b1, hy_f_fr1, hy_f_w2, hy_f_b2, hy_f_fr2,
                  hy_f_w3, hy_f_b3, hy_f_fr3, hy_f_w4, hy_skip,
                  cf_dw_w, cf_dw_b, cf_ln_g, cf_ln_b, xa_w_kv, w_out):
    L = x.shape[1]
    h = rmsnorm(x, norm_g)
    p = h @ w_in
    hy_u, hy_z, cf_a, cf_z, xa_q, xa_z = jnp.split(p, SPLITS, axis=-1)
    k2 = hyena_two_sided_filter(L, hy_f_w1, hy_f_b1, hy_f_fr1, hy_f_w2, hy_f_b2, hy_f_fr2,
                                hy_f_w3, hy_f_b3, hy_f_fr3, hy_f_w4)
    y_hy = hyena_branch(hy_u, hy_short_w, hy_short_b, k2, hy_skip) * jax.nn.silu(hy_z)
    y_cf = conformer_branch(cf_a, cf_dw_w, cf_dw_b, cf_ln_g, cf_ln_b) * jax.nn.silu(cf_z)
    y_xa = memory_attention(xa_q, rmsnorm(mem, mem_norm_g), xa_w_kv) * jax.nn.silu(xa_z)
    mix = jnp.concatenate([y_hy, y_cf, y_xa], axis=-1)
    return x + mix @ w_out


def run_trunk(x, mem, layer_params, final_g):
    for l in range(DEPTH):
        x = encoder_layer(x, mem, *[prm[l] for prm in layer_params])
    return rmsnorm(x, final_g)


def setup_inputs(seed: int = 0) -> dict:
    key = jax.random.key(seed)
    ks = iter(jax.random.split(key, 40))

    def nrm(shape, scale):
        return jax.random.normal(next(ks), shape, jnp.float32) * scale

    def gain(shape):
        return 1.0 + nrm(shape, 0.02)

    return {
        "x_prompt": nrm((BATCH, SEQ, D_MODEL), 1.0),
        "x_sample": nrm((DEC_BATCH, DEC_SEQ, D_MODEL), 1.0),
        "mem_prompt": nrm((BATCH, N_MEM, D_MODEL), 1.0),
        "mem_sample": nrm((DEC_BATCH, N_MEM, D_MODEL), 1.0),
        "norm_g": gain((DEPTH, D_MODEL)),
        "mem_norm_g": gain((DEPTH, D_MODEL)),
        "w_in": nrm((DEPTH, D_MODEL, D_IN), D_MODEL ** -0.5),
        "hy_short_w": nrm((DEPTH, SHORT_K, P_HY_U), SHORT_K ** -0.5),
        "hy_short_b": nrm((DEPTH, P_HY_U), 0.01),
        "hy_f_w1": nrm((DEPTH, FILT_EMB, FILT_HIDDEN), FILT_EMB ** -0.5),
        "hy_f_b1": nrm((DEPTH, FILT_HIDDEN), 0.1),
        "hy_f_fr1": 1.0 + nrm((DEPTH, FILT_HIDDEN), 0.05),
        "hy_f_w2": nrm((DEPTH, FILT_HIDDEN, FILT_HIDDEN), FILT_HIDDEN ** -0.5),
        "hy_f_b2": nrm((DEPTH, FILT_HIDDEN), 0.1),
        "hy_f_fr2": 1.0 + nrm((DEPTH, FILT_HIDDEN), 0.05),
        "hy_f_w3": nrm((DEPTH, FILT_HIDDEN, FILT_HIDDEN), FILT_HIDDEN ** -0.5),
        "hy_f_b3": nrm((DEPTH, FILT_HIDDEN), 0.1),
        "hy_f_fr3": 1.0 + nrm((DEPTH, FILT_HIDDEN), 0.05),
        "hy_f_w4": nrm((DEPTH, FILT_HIDDEN, 2 * D_HYENA), 0.03 * FILT_HIDDEN ** -0.5),
        "hy_skip": nrm((DEPTH, D_HYENA), 0.1),
        "cf_dw_w": nrm((DEPTH, CONF_K, D_CONF), CONF_K ** -0.5),
        "cf_dw_b": nrm((DEPTH, D_CONF), 0.01),
        "cf_ln_g": gain((DEPTH, D_CONF)),
        "cf_ln_b": nrm((DEPTH, D_CONF), 0.01),
        "xa_w_kv": nrm((DEPTH, D_MODEL, 2 * D_XATTN), D_MODEL ** -0.5),
        "w_out": nrm((DEPTH, D_MIX, D_MODEL), D_MIX ** -0.5),
        "final_g": gain((D_MODEL,)),
    }


def reference(x_prompt, x_sample, mem_prompt, mem_sample, norm_g, mem_norm_g, w_in,
              hy_short_w, hy_short_b, hy_f_w1, hy_f_b1, hy_f_fr1, hy_f_w2, hy_f_b2, hy_f_fr2,
              hy_f_w3, hy_f_b3, hy_f_fr3, hy_f_w4, hy_skip, cf_dw_w, cf_dw_b, cf_ln_g, cf_ln_b,
              xa_w_kv, w_out, final_g):
    layer_params = (norm_g, mem_norm_g, w_in, hy_short_w, hy_short_b,
                    hy_f_w1, hy_f_b1, hy_f_fr1, hy_f_w2, hy_f_b2, hy_f_fr2,
                    hy_f_w3, hy_f_b3, hy_f_fr3, hy_f_w4, hy_skip,
                    cf_dw_w, cf_dw_b, cf_ln_g, cf_ln_b, xa_w_kv, w_out)
    y_prompt = run_trunk(x_prompt, mem_prompt, layer_params, final_g)
    y_sample = run_trunk(x_sample, mem_sample, layer_params, final_g)
    return (y_prompt, y_sample)
```

```python
import functools
import math

import jax
import jax.numpy as jnp
from jax import lax
from jax.experimental import pallas as pl
from jax.experimental.pallas import tpu as pltpu

f32 = jnp.float32
bf16 = jnp.bfloat16

D_MODEL = 1024
SEQ = 8192
DEPTH = 4
N_MEM = 256
HEAD_DIM = 64
D_HYENA = 512
D_CONF = 256
D_XATTN = 256
N_XHEADS = D_XATTN // HEAD_DIM
SHORT_K = 3
CONF_K = 31
FILT_BANDS = 16
FILT_EMB = 1 + 2 * FILT_BANDS
FILT_HIDDEN = 64
DECAY_TARGET = 1e-2
FAST_DECAY_PCT = 0.3
SLOW_DECAY_PCT = 1.5
EPS = 1e-6
P_HY_U = 3 * D_HYENA
D_IN = P_HY_U + D_HYENA + 2 * D_CONF + D_CONF + D_XATTN + D_XATTN
C_HZ = P_HY_U
C_CA = C_HZ + D_HYENA
C_CZ = C_CA + 2 * D_CONF
C_XQ = C_CZ + D_CONF
C_XZ = C_XQ + D_XATTN

LANES = 128
N1 = 128
NFFT = N1 * N1
NI = SEQ // N1
K1 = N1 // 2 + 1
K1P = 72
SROWS = 2 * K1P
PITCH = 152
TILE = 1024
NTILE = SEQ // TILE
IL = TILE // N1
NSLAB = D_HYENA // LANES
HALO = 16
EXT = TILE + 2 * HALO
VMEM_LIMIT = 60 * 1024 * 1024


def _split(x):
    hi = x.astype(bf16)
    lo = (x - hi.astype(f32)).astype(bf16)
    return hi, lo


def _rep(shape):
    return pl.BlockSpec(shape, lambda *_: (0,) * len(shape), pipeline_mode=pl.Buffered(1))


def _dft_tables():
    k1 = jnp.arange(K1P, dtype=jnp.int32)
    i = jnp.arange(NI, dtype=jnp.int32)
    j = jnp.arange(N1, dtype=jnp.int32)
    valid = (k1 < K1).astype(f32)
    scale = 2.0 * math.pi / NFFT
    ph = (j[:, None, None] * k1[None, :, None] + N1 * k1[None, :, None] * i[None, None, :]) % NFFT
    ang = ph.astype(f32) * scale
    g = jnp.concatenate([jnp.cos(ang) * valid[None, :, None], -jnp.sin(ang) * valid[None, :, None]], axis=1)
    g_hi, g_lo = _split(g)
    g3 = jnp.concatenate([g_hi, g_hi, g_lo, jnp.zeros((N1, SROWS, 4 * NI - 3 * NI), bf16)], axis=2)
    cw = jnp.where((k1 == 0) | (k1 == N1 // 2), 1.0, 2.0) * valid / NFFT
    ph4 = (N1 * i[None, :, None] * k1[None, None, :] + j[:, None, None] * k1[None, None, :]) % NFFT
    ang4 = ph4.astype(f32) * scale
    h = jnp.concatenate([jnp.cos(ang4) * cw, -jnp.sin(ang4) * cw], axis=2)
    h_hi, h_lo = _split(h)
    h3 = jnp.concatenate([h_hi, h_hi, h_lo, jnp.zeros((N1, NI, 512 - 3 * SROWS), bf16)], axis=2)
    th = ((j[:, None] * j[None, :]) % N1).astype(f32) * (2.0 * math.pi / N1)
    c, s = jnp.cos(th), jnp.sin(th)
    fwd = jnp.concatenate([jnp.concatenate([c, s], 1), jnp.concatenate([-s, c], 1)], 0)
    inv = jnp.concatenate([jnp.concatenate([c, -s], 1), jnp.concatenate([s, c], 1)], 0)

    def cat3(m):
        m_hi, m_lo = _split(m)
        return jnp.concatenate([m_hi, m_hi, m_lo], axis=1)

    return g3, h3, cat3(fwd), cat3(inv)


def _positional():
    t = jnp.linspace(0.0, 1.0, SEQ, dtype=f32)[:, None]
    n = jnp.arange(SEQ, dtype=f32)[:, None]
    bands = jnp.linspace(1e-4, FILT_BANDS - 1, FILT_BANDS, dtype=f32)[None, :]
    ang = (2.0 * math.pi / SEQ) * bands * n
    z = jnp.concatenate([t, jnp.cos(ang), -jnp.sin(ang)], axis=-1)
    z = jnp.pad(z, ((0, 0), (0, LANES - FILT_EMB)))
    z = z.reshape(NTILE, IL, N1, LANES).transpose(0, 2, 1, 3).reshape(NTILE, TILE, LANES)
    max_decay = math.log(DECAY_TARGET) / FAST_DECAY_PCT
    min_decay = math.log(DECAY_TARGET) / SLOW_DECAY_PCT
    deltas = jnp.abs(jnp.linspace(min_decay, max_decay, D_HYENA, dtype=f32))
    return z, jnp.concatenate([deltas, deltas])[None, :]


def _filter_kernel(z_ref, w1_ref, b1_ref, fr1_ref, w2_ref, b2_ref, fr2_ref, w3_ref, b3_ref, fr3_ref,
                   w4_ref, dl_ref, o_ref):
    hp = lax.Precision.HIGHEST
    z = z_ref[...]
    t = z[:, 0:1]
    h = jnp.sin(fr1_ref[...] * (jnp.dot(z, w1_ref[...], precision=hp, preferred_element_type=f32) + b1_ref[...]))
    h = jnp.sin(fr2_ref[...] * (jnp.dot(h, w2_ref[...], precision=hp, preferred_element_type=f32) + b2_ref[...]))
    h = jnp.sin(fr3_ref[...] * (jnp.dot(h, w3_ref[...], precision=hp, preferred_element_type=f32) + b3_ref[...]))
    h = jnp.dot(h, w4_ref[...], precision=hp, preferred_element_type=f32)
    h = h * jnp.exp(-t * dl_ref[...])
    col = lax.broadcasted_iota(jnp.int32, h.shape, 1)
    o_ref[...] = jnp.where((col >= D_HYENA) & (t <= 0.0), 0.0, h)


def _filter(z, deltas, w1, b1, fr1, w2, b2, fr2, w3, b3, fr3, w4):
    hp = LANES - FILT_HIDDEN
    w1p = jnp.pad(w1, ((0, LANES - FILT_EMB), (0, hp)))
    w2p = jnp.pad(w2, ((0, hp), (0, hp)))
    w3p = jnp.pad(w3, ((0, hp), (0, hp)))
    w4p = jnp.pad(w4, ((0, hp), (0, 0)))
    vec = lambda v: jnp.pad(v, (0, hp))[None, :]
    args = (z, w1p, vec(b1), vec(fr1), w2p, vec(b2), vec(fr2), w3p, vec(b3), vec(fr3), w4p, deltas)
    specs = [pl.BlockSpec((None, TILE, LANES), lambda t: (t, 0, 0))] + [_rep(a.shape) for a in args[1:]]
    return pl.pallas_call(
        _filter_kernel,
        out_shape=jax.ShapeDtypeStruct((NTILE, TILE, 2 * D_HYENA), f32),
        grid=(NTILE,),
        in_specs=specs,
        out_specs=pl.BlockSpec((None, TILE, 2 * D_HYENA), lambda t: (t, 0, 0)),
        name="hyena_filter",
    )(*args)


def _stage1(src_ref, g3_ref, s_ref):
    pad = jnp.zeros((NI, LANES), bf16)

    def body(j, c):
        x = src_ref[:, pl.ds(pl.multiple_of(j * IL, IL), IL), :].reshape(NI, LANES)
        hi, lo = _split(x)
        rhs = jnp.concatenate([hi, lo, hi, pad], axis=0)
        a = jnp.dot(g3_ref[j], rhs, preferred_element_type=f32)
        s_ref[pl.ds(pl.multiple_of(j * PITCH, 8), SROWS), :] = a
        return c

    lax.fori_loop(0, N1, body, 0)


def _dot3(f_ref, a):
    hi, lo = _split(a)
    return jnp.dot(f_ref[...], jnp.concatenate([hi, lo, hi], axis=0), preferred_element_type=f32)


def _load_k1(s_ref, k):
    ar = s_ref[pl.ds(k, N1, stride=PITCH), :]
    ai = s_ref[pl.ds(K1P + k, N1, stride=PITCH), :]
    return jnp.concatenate([ar, ai], axis=0)


def _spectrum_kernel(hf_ref, hb_ref, g3_ref, f2_ref, o_ref, s_ref):
    _stage1(hf_ref, g3_ref, s_ref)

    def fwd(k, c):
        o_ref[pl.ds(pl.multiple_of(k * 2 * N1, 2 * N1), 2 * N1), :] = _dot3(f2_ref, _load_k1(s_ref, k))
        return c

    lax.fori_loop(0, K1, fwd, 0)
    _stage1(hb_ref, g3_ref, s_ref)

    def bwd(k, c):
        a = _dot3(f2_ref, _load_k1(s_ref, k))
        r0 = pl.multiple_of(k * 2 * N1, 2 * N1)
        o_ref[pl.ds(r0, N1), :] = o_ref[pl.ds(r0, N1), :] + a[:N1]
        o_ref[pl.ds(r0 + N1, N1), :] = o_ref[pl.ds(r0 + N1, N1), :] - a[N1:]
        return c

    lax.fori_loop(0, K1, bwd, 0)


def _spectrum(h, g3, f2):
    return pl.pallas_call(
        _spectrum_kernel,
        out_shape=jax.ShapeDtypeStruct((NSLAB, K1 * 2 * N1, LANES), f32),
        grid=(NSLAB,),
        in_specs=[
            pl.BlockSpec((NTILE, TILE, LANES), lambda c: (0, 0, c)),
            pl.BlockSpec((NTILE, TILE, LANES), lambda c: (0, 0, NSLAB + c)),
            _rep(g3.shape),
            _rep(f2.shape),
        ],
        out_specs=pl.BlockSpec((None, K1 * 2 * N1, LANES), lambda c: (c, 0, 0)),
        scratch_shapes=[pltpu.VMEM((N1 * PITCH, LANES), f32)],
        compiler_params=pltpu.CompilerParams(vmem_limit_bytes=VMEM_LIMIT),
        name="hyena_spectrum",
    )(h, h, g3, f2)


def _conv_kernel(w_ref, kf_ref, skip_ref, g3_ref, h3_ref, f2_ref, f2i_ref, o_ref, s_ref):
    _stage1(w_ref, g3_ref, s_ref)

    def mid(k, c):
        a = _dot3(f2_ref, _load_k1(s_ref, k))
        kf = kf_ref[pl.ds(pl.multiple_of(k * 2 * N1, 2 * N1), 2 * N1), :]
        xr, xi, kr, ki = a[:N1], a[N1:], kf[:N1], kf[N1:]
        y = jnp.concatenate([xr * kr - xi * ki, xr * ki + xi * kr], axis=0)
        b = _dot3(f2i_ref, y)
        s_ref[pl.ds(k, N1, stride=PITCH), :] = b[:N1]
        s_ref[pl.ds(K1P + k, N1, stride=PITCH), :] = b[N1:]
        return c

    lax.fori_loop(0, K1, mid, 0)
    pad = jnp.zeros((512 - 3 * SROWS, LANES), bf16)
    skip = skip_ref[...]

    def last(j, c):
        b = s_ref[pl.ds(pl.multiple_of(j * PITCH, 8), SROWS), :]
        hi, lo = _split(b)
        y = jnp.dot(h3_ref[j], jnp.concatenate([hi, lo, hi, pad], axis=0), preferred_element_type=f32)
        rows = pl.ds(pl.multiple_of(j * IL, IL), IL)
        x = w_ref[:, rows, :].reshape(NI, LANES)
        o_ref[:, rows, :] = (y + x * skip).reshape(NTILE, IL, LANES)
        return c

    lax.fori_loop(0, N1, last, 0)


def _long_conv(wp, kf, skip, g3, h3, f2, f2i):
    nb = wp.shape[0]
    blk = pl.BlockSpec((None, NTILE, None, TILE, LANES), lambda c, b: (b, 0, c, 0, 0))
    return pl.pallas_call(
        _conv_kernel,
        out_shape=jax.ShapeDtypeStruct(wp.shape, f32),
        grid=(NSLAB, nb),
        in_specs=[
            blk,
            pl.BlockSpec((None, K1 * 2 * N1, LANES), lambda c, b: (c, 0, 0), pipeline_mode=pl.Buffered(1)),
            pl.BlockSpec((None, 1, LANES), lambda c, b: (c, 0, 0)),
            _rep(g3.shape),
            _rep(h3.shape),
            _rep(f2.shape),
            _rep(f2i.shape),
        ],
        out_specs=blk,
        scratch_shapes=[pltpu.VMEM((N1 * PITCH, LANES), f32)],
        compiler_params=pltpu.CompilerParams(vmem_limit_bytes=VMEM_LIMIT),
        name="hyena_long_conv",
    )(wp, kf, skip, g3, h3, f2, f2i)


def _rms(v, g):
    return v * lax.rsqrt(jnp.mean(v * v, axis=-1, keepdims=True) + EPS) * g


def _kv_kernel(mem_ref, g_ref, wkt_ref, wv_ref, kbd_ref, vbd_ref):
    mn = _rms(mem_ref[...], g_ref[...]).astype(bf16)
    kt = lax.dot_general(wkt_ref[...], mn, (((1,), (1,)), ((), ())), preferred_element_type=f32)
    kt = kt * (HEAD_DIM ** -0.5)
    v = jnp.dot(mn, wv_ref[...], preferred_element_type=f32)
    kb = jnp.concatenate([kt] * N_XHEADS, axis=1)
    hd_bits, mem_bits = HEAD_DIM.bit_length() - 1, N_MEM.bit_length() - 1
    r = lax.broadcasted_iota(jnp.int32, kb.shape, 0) >> hd_bits
    c = lax.broadcasted_iota(jnp.int32, kb.shape, 1) >> mem_bits
    kbd_ref[...] = jnp.where(r == c, kb, 0.0).astype(bf16)
    vb = jnp.concatenate([v] * N_XHEADS, axis=0)
    r = lax.broadcasted_iota(jnp.int32, vb.shape, 0) >> mem_bits
    c = lax.broadcasted_iota(jnp.int32, vb.shape, 1) >> hd_bits
    vbd_ref[...] = jnp.where(r == c, vb, 0.0).astype(bf16)


def _memory_kv(mem, g, wkt, wv):
    nb = mem.shape[0]
    return pl.pallas_call(
        _kv_kernel,
        out_shape=(jax.ShapeDtypeStruct((nb, D_XATTN, N_XHEADS * N_MEM), bf16),
                   jax.ShapeDtypeStruct((nb, N_XHEADS * N_MEM, D_XATTN), bf16)),
        grid=(nb,),
        in_specs=[pl.BlockSpec((None, N_MEM, D_MODEL), lambda b: (b, 0, 0)), _rep(g.shape), _rep(wkt.shape),
                  _rep(wv.shape)],
        out_specs=(pl.BlockSpec((None, D_XATTN, N_XHEADS * N_MEM), lambda b: (b, 0, 0)),
                   pl.BlockSpec((None, N_XHEADS * N_MEM, D_XATTN), lambda b: (b, 0, 0))),
        name="memory_kv",
    )(mem, g, wkt, wv)


def _silu(v):
    return v * jax.nn.sigmoid(v)


def _mix_kernel(x_ref, xp_ref, xn_ref, ng_ref, win_ref, sw_ref, sb_ref, dw_ref, db_ref, lng_ref, lnb_ref,
                kbd_ref, vbd_ref, w_ref, g_ref, y_ref, hext, pbuf, glu):
    t = pl.program_id(1)
    ng = ng_ref[...]
    hext[0:HALO, :] = _rms(xp_ref[...] * (t > 0).astype(f32), ng).astype(bf16)
    for r in range(IL):
        hext[HALO + N1 * r:HALO + N1 * (r + 1), :] = _rms(x_ref[N1 * r:N1 * (r + 1), :], ng).astype(bf16)
    hext[HALO + TILE:EXT, :] = _rms(xn_ref[...] * (t < NTILE - 1).astype(f32), ng).astype(bf16)

    cg = 256
    for c0 in range(0, D_IN, cg):
        pbuf[:, c0:c0 + cg] = jnp.dot(hext[...], win_ref[:, c0:c0 + cg], preferred_element_type=f32)

    rc = 176
    for r0 in range(0, EXT, rc):
        glu[r0:r0 + rc, :] = (pbuf[r0:r0 + rc, C_CA:C_CA + D_CONF]
                              * jax.nn.sigmoid(pbuf[r0:r0 + rc, C_CA + D_CONF:C_CA + 2 * D_CONF]))

    kbd = kbd_ref[...]
    vbd = vbd_ref[...]
    for r in range(IL):
        e0 = HALO + N1 * r
        rows = slice(e0, e0 + N1)
        orow = slice(N1 * r, N1 * (r + 1))

        def sconv(c0):
            acc = sb_ref[:, c0:c0 + D_HYENA]
            for k in range(SHORT_K):
                acc = acc + pbuf[e0 - 1 + k:e0 - 1 + k + N1, c0:c0 + D_HYENA] * sw_ref[k:k + 1, c0:c0 + D_HYENA]
            return acc

        x0 = sconv(0)
        wv = sconv(2 * D_HYENA) * sconv(D_HYENA)
        for s in range(NSLAB):
            w_ref[s, pl.ds(r, N1, stride=IL), :] = wv[:, LANES * s:LANES * (s + 1)]
        g_ref[orow, :] = x0 * _silu(pbuf[rows, C_HZ:C_HZ + D_HYENA])

        acc = db_ref[...]
        for k in range(CONF_K):
            lo = e0 - CONF_K // 2 + k
            acc = acc + glu[lo:lo + N1, :] * dw_ref[k:k + 1, :]
        mu = jnp.mean(acc, axis=-1, keepdims=True)
        xc = acc - mu
        ln = xc * lax.rsqrt(jnp.mean(xc * xc, axis=-1, keepdims=True) + EPS) * lng_ref[...] + lnb_ref[...]
        y_ref[orow, 0:D_CONF] = (_silu(ln) * _silu(pbuf[rows, C_CZ:C_CZ + D_CONF])).astype(bf16)

        q = pbuf[rows, C_XQ:C_XQ + D_XATTN].astype(bf16)
        s_all = jnp.dot(q, kbd, preferred_element_type=f32)
        ps = []
        for h in range(N_XHEADS):
            sh = s_all[:, N_MEM * h:N_MEM * (h + 1)]
            e = jnp.exp(sh - jnp.max(sh, axis=-1, keepdims=True))
            ps.append((e / jnp.sum(e, axis=-1, keepdims=True)).astype(bf16))
        o = jnp.dot(jnp.concatenate(ps, axis=1), vbd, preferred_element_type=f32)
        y_ref[orow, D_CONF:D_CONF + D_XATTN] = (o * _silu(pbuf[rows, C_XZ:C_XZ + D_XATTN])).astype(bf16)


def _mix(x, ng, win, sw, sb, dw, db, lng, lnb, kbd, vbd):
    nb = x.shape[0]
    hb = TILE // HALO
    nhb = SEQ // HALO
    small = [ng, win, sw, sb, dw, db, lng, lnb]
    return pl.pallas_call(
        _mix_kernel,
        out_shape=(jax.ShapeDtypeStruct((nb, NTILE, NSLAB, TILE, LANES), f32),
                   jax.ShapeDtypeStruct((nb, SEQ, D_HYENA), f32),
                   jax.ShapeDtypeStruct((nb, SEQ, D_CONF + D_XATTN), bf16)),
        grid=(nb, NTILE),
        in_specs=[
            pl.BlockSpec((None, TILE, D_MODEL), lambda b, t: (b, t, 0)),
            pl.BlockSpec((None, HALO, D_MODEL), lambda b, t: (b, jnp.maximum(t * hb - 1, 0), 0)),
            pl.BlockSpec((None, HALO, D_MODEL), lambda b, t: (b, jnp.minimum((t + 1) * hb, nhb - 1), 0)),
        ] + [_rep(a.shape) for a in small] + [
            pl.BlockSpec((None, D_XATTN, N_XHEADS * N_MEM), lambda b, t: (b, 0, 0)),
            pl.BlockSpec((None, N_XHEADS * N_MEM, D_XATTN), lambda b, t: (b, 0, 0)),
        ],
        out_specs=(pl.BlockSpec((None, None, NSLAB, TILE, LANES), lambda b, t: (b, t, 0, 0, 0)),
                   pl.BlockSpec((None, TILE, D_HYENA), lambda b, t: (b, t, 0)),
                   pl.BlockSpec((None, TILE, D_CONF + D_XATTN), lambda b, t: (b, t, 0))),
        scratch_shapes=[pltpu.VMEM((EXT, D_MODEL), bf16), pltpu.VMEM((EXT, D_IN), f32),
                        pltpu.VMEM((EXT, D_CONF), f32)],
        compiler_params=pltpu.CompilerParams(vmem_limit_bytes=VMEM_LIMIT),
        name="token_mix",
    )(x, x, x, *small, kbd, vbd)


def _out_kernel(x_ref, g_ref, c_ref, y_ref, wout_ref, fg_ref, o_ref, mix, *, final):
    for r in range(IL):
        rows = slice(N1 * r, N1 * (r + 1))
        for s in range(NSLAB):
            cols = slice(LANES * s, LANES * (s + 1))
            conv = c_ref[s, pl.ds(r, N1, stride=IL), :]
            mix[rows, cols] = (g_ref[rows, cols] * conv).astype(bf16)
    mix[:, D_HYENA:] = y_ref[...]
    xn = x_ref[...] + jnp.dot(mix[...], wout_ref[...], preferred_element_type=f32)
    o_ref[...] = _rms(xn, fg_ref[...]) if final else xn


def _out_proj(x, g, cp, y, wout, fg, final):
    nb = x.shape[0]
    return pl.pallas_call(
        functools.partial(_out_kernel, final=final),
        out_shape=jax.ShapeDtypeStruct(x.shape, f32),
        grid=(nb, NTILE),
        in_specs=[
            pl.BlockSpec((None, TILE, D_MODEL), lambda b, t: (b, t, 0)),
            pl.BlockSpec((None, TILE, D_HYENA), lambda b, t: (b, t, 0)),
            pl.BlockSpec((None, None, NSLAB, TILE, LANES), lambda b, t: (b, t, 0, 0, 0)),
            pl.BlockSpec((None, TILE, D_CONF + D_XATTN), lambda b, t: (b, t, 0)),
            _rep(wout.shape),
            _rep(fg.shape),
        ],
        out_specs=pl.BlockSpec((None, TILE, D_MODEL), lambda b, t: (b, t, 0)),
        scratch_shapes=[pltpu.VMEM((TILE, D_MODEL), bf16)],
        compiler_params=pltpu.CompilerParams(vmem_limit_bytes=VMEM_LIMIT),
        name="out_proj",
    )(x, g, cp, y, wout, fg)


def kernel(x_prompt, x_sample, mem_prompt, mem_sample, norm_g, mem_norm_g, w_in, hy_short_w, hy_short_b, hy_f_w1, hy_f_b1, hy_f_fr1, hy_f_w2, hy_f_b2, hy_f_fr2, hy_f_w3, hy_f_b3, hy_f_fr3, hy_f_w4, hy_skip, cf_dw_w, cf_dw_b, cf_ln_g, cf_ln_b, xa_w_kv, w_out, final_g):
    nbp = x_prompt.shape[0]
    x = jnp.concatenate([x_prompt, x_sample], axis=0)
    mem = jnp.concatenate([mem_prompt, mem_sample], axis=0)
    g3, h3, f2, f2i = _dft_tables()
    z, deltas = _positional()
    row = lambda v: v[None, :]
    for l in range(DEPTH):
        h = _filter(z, deltas, hy_f_w1[l], hy_f_b1[l], hy_f_fr1[l], hy_f_w2[l], hy_f_b2[l], hy_f_fr2[l],
                    hy_f_w3[l], hy_f_b3[l], hy_f_fr3[l], hy_f_w4[l])
        kf = _spectrum(h, g3, f2)
        kbd, vbd = _memory_kv(mem, row(mem_norm_g[l]), xa_w_kv[l][:, :D_XATTN].T.astype(bf16),
                              xa_w_kv[l][:, D_XATTN:].astype(bf16))
        wp, g, y = _mix(x, row(norm_g[l]), w_in[l].astype(bf16), hy_short_w[l], row(hy_short_b[l]), cf_dw_w[l],
                        row(cf_dw_b[l]), row(cf_ln_g[l]), row(cf_ln_b[l]), kbd, vbd)
        cp = _long_conv(wp, kf, hy_skip[l].reshape(NSLAB, 1, LANES), g3, h3, f2, f2i)
        x = _out_proj(x, g, cp, y, w_out[l].astype(bf16), row(final_g), final=(l == DEPTH - 1))
    return x[:nbp], x[nbp:]
```

```python
import functools
import math

import jax
import jax.numpy as jnp
from jax import lax
from jax.experimental import pallas as pl
from jax.experimental.pallas import tpu as pltpu

f32 = jnp.float32
bf16 = jnp.bfloat16

D_MODEL = 1024
SEQ = 8192
DEPTH = 4
N_MEM = 256
HEAD_DIM = 64
D_HYENA = 512
D_CONF = 256
D_XATTN = 256
N_XHEADS = D_XATTN // HEAD_DIM
SHORT_K = 3
CONF_K = 31
FILT_BANDS = 16
FILT_EMB = 1 + 2 * FILT_BANDS
FILT_HIDDEN = 64
DECAY_TARGET = 1e-2
FAST_DECAY_PCT = 0.3
SLOW_DECAY_PCT = 1.5
EPS = 1e-6
P_HY_U = 3 * D_HYENA
D_IN = P_HY_U + D_HYENA + 2 * D_CONF + D_CONF + D_XATTN + D_XATTN
C_HZ = P_HY_U
C_CA = C_HZ + D_HYENA
C_CZ = C_CA + 2 * D_CONF
C_XQ = C_CZ + D_CONF
C_XZ = C_XQ + D_XATTN

LANES = 128
N1 = 128
NFFT = N1 * N1
NI = SEQ // N1
K1 = N1 // 2 + 1
K1P = 72
KPAIRS = (K1 + 1) // 2
SROWS = 2 * K1P
PITCH = 152
TILE = 1024
NTILE = SEQ // TILE
IL = TILE // N1
NSLAB = D_HYENA // LANES
HALO = 16
EXT = TILE + 2 * HALO
VMEM_LIMIT = 60 * 1024 * 1024


def _split(x):
    hi = x.astype(bf16)
    lo = (x - hi.astype(f32)).astype(bf16)
    return hi, lo


def _rep(shape):
    return pl.BlockSpec(shape, lambda *_: (0,) * len(shape), pipeline_mode=pl.Buffered(1))


def _dft_tables():
    k1 = jnp.arange(K1P, dtype=jnp.int32)
    i = jnp.arange(NI, dtype=jnp.int32)
    j = jnp.arange(N1, dtype=jnp.int32)
    valid = (k1 < K1).astype(f32)
    scale = 2.0 * math.pi / NFFT
    ph = (j[:, None, None] * k1[None, :, None] + N1 * k1[None, :, None] * i[None, None, :]) % NFFT
    ang = ph.astype(f32) * scale
    g = jnp.concatenate([jnp.cos(ang) * valid[None, :, None], -jnp.sin(ang) * valid[None, :, None]], axis=1)
    g_hi, g_lo = _split(g)
    g3 = jnp.concatenate([g_hi, g_hi, g_lo, jnp.zeros((N1, SROWS, 4 * NI - 3 * NI), bf16)], axis=2)
    cw = jnp.where((k1 == 0) | (k1 == N1 // 2), 1.0, 2.0) * valid / NFFT
    ph4 = (N1 * i[None, :, None] * k1[None, None, :] + j[:, None, None] * k1[None, None, :]) % NFFT
    ang4 = ph4.astype(f32) * scale
    h = jnp.concatenate([jnp.cos(ang4) * cw, -jnp.sin(ang4) * cw], axis=2)
    h1 = jnp.concatenate([h.astype(bf16), jnp.zeros((N1, NI, 2 * N1 - SROWS), bf16)], axis=2)
    th = ((j[:, None] * j[None, :]) % N1).astype(f32) * (2.0 * math.pi / N1)
    c, s = jnp.cos(th), jnp.sin(th)
    fwd = jnp.concatenate([jnp.concatenate([c, s], 1), jnp.concatenate([-s, c], 1)], 0)
    inv = jnp.concatenate([jnp.concatenate([c, -s], 1), jnp.concatenate([s, c], 1)], 0)

    def cat3(m):
        m_hi, m_lo = _split(m)
        return jnp.concatenate([m_hi, m_hi, m_lo], axis=1)

    return g3, h1, cat3(fwd), fwd.astype(bf16), inv.astype(bf16)


def _positional():
    t = jnp.linspace(0.0, 1.0, SEQ, dtype=f32)[:, None]
    n = jnp.arange(SEQ, dtype=f32)[:, None]
    bands = jnp.linspace(1e-4, FILT_BANDS - 1, FILT_BANDS, dtype=f32)[None, :]
    ang = (2.0 * math.pi / SEQ) * bands * n
    z = jnp.concatenate([t, jnp.cos(ang), -jnp.sin(ang)], axis=-1)
    z = jnp.pad(z, ((0, 0), (0, LANES - FILT_EMB)))
    z = z.reshape(NTILE, IL, N1, LANES).transpose(0, 2, 1, 3).reshape(NTILE, TILE, LANES)
    max_decay = math.log(DECAY_TARGET) / FAST_DECAY_PCT
    min_decay = math.log(DECAY_TARGET) / SLOW_DECAY_PCT
    deltas = jnp.abs(jnp.linspace(min_decay, max_decay, D_HYENA, dtype=f32))
    return z, jnp.concatenate([deltas, deltas])[None, :]


def _filter_kernel(z_ref, w1_ref, b1_ref, fr1_ref, w2_ref, b2_ref, fr2_ref, w3_ref, b3_ref, fr3_ref,
                   w4_ref, dl_ref, o_ref):
    hp = lax.Precision.HIGHEST
    z = z_ref[...]
    t = z[:, 0:1]
    h = jnp.sin(fr1_ref[...] * (jnp.dot(z, w1_ref[...], precision=hp, preferred_element_type=f32) + b1_ref[...]))
    h = jnp.sin(fr2_ref[...] * (jnp.dot(h, w2_ref[...], precision=hp, preferred_element_type=f32) + b2_ref[...]))
    h = jnp.sin(fr3_ref[...] * (jnp.dot(h, w3_ref[...], precision=hp, preferred_element_type=f32) + b3_ref[...]))
    h = jnp.dot(h, w4_ref[...], precision=hp, preferred_element_type=f32)
    h = h * jnp.exp(-t * dl_ref[...])
    col = lax.broadcasted_iota(jnp.int32, h.shape, 1)
    o_ref[...] = jnp.where((col >= D_HYENA) & (t <= 0.0), 0.0, h)


def _filter(z, deltas, w1, b1, fr1, w2, b2, fr2, w3, b3, fr3, w4):
    hp = LANES - FILT_HIDDEN
    w1p = jnp.pad(w1, ((0, LANES - FILT_EMB), (0, hp)))
    w2p = jnp.pad(w2, ((0, hp), (0, hp)))
    w3p = jnp.pad(w3, ((0, hp), (0, hp)))
    w4p = jnp.pad(w4, ((0, hp), (0, 0)))
    vec = lambda v: jnp.pad(v, (0, hp))[None, :]
    args = (z, w1p, vec(b1), vec(fr1), w2p, vec(b2), vec(fr2), w3p, vec(b3), vec(fr3), w4p, deltas)
    specs = [pl.BlockSpec((None, TILE, LANES), lambda t: (t, 0, 0))] + [_rep(a.shape) for a in args[1:]]
    return pl.pallas_call(
        _filter_kernel,
        out_shape=jax.ShapeDtypeStruct((NTILE, TILE, 2 * D_HYENA), f32),
        grid=(NTILE,),
        in_specs=specs,
        out_specs=pl.BlockSpec((None, TILE, 2 * D_HYENA), lambda t: (t, 0, 0)),
        name="hyena_filter",
    )(*args)


def _stage1(src_ref, g3_ref, s_ref):
    pad = jnp.zeros((NI, LANES), bf16)

    def body(j, c):
        x = src_ref[:, pl.ds(pl.multiple_of(j * IL, IL), IL), :].reshape(NI, LANES)
        hi, lo = _split(x)
        rhs = jnp.concatenate([hi, lo, hi, pad], axis=0)
        a = jnp.dot(g3_ref[j], rhs, preferred_element_type=f32)
        s_ref[pl.ds(pl.multiple_of(j * PITCH, 8), SROWS), :] = a
        return c

    lax.fori_loop(0, N1, body, 0, unroll=8)


def _dot3(f_ref, a):
    hi, lo = _split(a)
    return jnp.dot(f_ref[...], jnp.concatenate([hi, lo, hi], axis=0), preferred_element_type=f32)


def _load_pair(s_ref, kk):
    k = 2 * kk
    parts = [jnp.concatenate([s_ref[pl.ds(off + k, N1, stride=PITCH), :],
                              s_ref[pl.ds(off + k + 1, N1, stride=PITCH), :]], axis=1) for off in (0, K1P)]
    return jnp.concatenate(parts, axis=0)


def _pair_rows(kk):
    return pl.ds(pl.multiple_of(kk * 2 * N1, 2 * N1), 2 * N1)


def _spectrum_kernel(hf_ref, hb_ref, g3_ref, f2_ref, o_ref, s_ref):
    _stage1(hf_ref, g3_ref, s_ref)

    def fwd(kk, c):
        o_ref[_pair_rows(kk), :] = _dot3(f2_ref, _load_pair(s_ref, kk))
        return c

    lax.fori_loop(0, KPAIRS, fwd, 0, unroll=2)
    _stage1(hb_ref, g3_ref, s_ref)

    def bwd(kk, c):
        a = _dot3(f2_ref, _load_pair(s_ref, kk))
        r0 = pl.multiple_of(kk * 2 * N1, 2 * N1)
        o_ref[pl.ds(r0, N1), :] = o_ref[pl.ds(r0, N1), :] + a[:N1]
        o_ref[pl.ds(r0 + N1, N1), :] = o_ref[pl.ds(r0 + N1, N1), :] - a[N1:]
        return c

    lax.fori_loop(0, KPAIRS, bwd, 0, unroll=2)


def _spectrum(h, g3, f2):
    return pl.pallas_call(
        _spectrum_kernel,
        out_shape=jax.ShapeDtypeStruct((NSLAB, KPAIRS * 2 * N1, 2 * LANES), f32),
        grid=(NSLAB,),
        in_specs=[
            pl.BlockSpec((NTILE, TILE, LANES), lambda c: (0, 0, c)),
            pl.BlockSpec((NTILE, TILE, LANES), lambda c: (0, 0, NSLAB + c)),
            _rep(g3.shape),
            _rep(f2.shape),
        ],
        out_specs=pl.BlockSpec((None, KPAIRS * 2 * N1, 2 * LANES), lambda c: (c, 0, 0)),
        scratch_shapes=[pltpu.VMEM((N1 * PITCH, LANES), f32)],
        compiler_params=pltpu.CompilerParams(vmem_limit_bytes=VMEM_LIMIT),
        name="hyena_spectrum",
    )(h, h, g3, f2)


def _conv_kernel(w_ref, kf_ref, skip_ref, g3_ref, h1_ref, f2_ref, f2i_ref, o_ref, s_ref):
    _stage1(w_ref, g3_ref, s_ref)

    def mid(kk, c):
        a = jnp.dot(f2_ref[...], _load_pair(s_ref, kk).astype(bf16), preferred_element_type=f32)
        kf = kf_ref[_pair_rows(kk), :]
        xr, xi, kr, ki = a[:N1], a[N1:], kf[:N1], kf[N1:]
        y = jnp.concatenate([xr * kr - xi * ki, xr * ki + xi * kr], axis=0)
        b = jnp.dot(f2i_ref[...], y.astype(bf16), preferred_element_type=f32)
        k = 2 * kk
        for p in range(2):
            s_ref[pl.ds(k + p, N1, stride=PITCH), :] = b[:N1, LANES * p:LANES * (p + 1)]
            s_ref[pl.ds(K1P + k + p, N1, stride=PITCH), :] = b[N1:, LANES * p:LANES * (p + 1)]
        return c

    lax.fori_loop(0, KPAIRS, mid, 0, unroll=3)
    pad = jnp.zeros((2 * N1 - SROWS, LANES), bf16)
    skip = skip_ref[...]

    def last(j, c):
        b = s_ref[pl.ds(pl.multiple_of(j * PITCH, 8), SROWS), :].astype(bf16)
        y = jnp.dot(h1_ref[j], jnp.concatenate([b, pad], axis=0), preferred_element_type=f32)
        rows = pl.ds(pl.multiple_of(j * IL, IL), IL)
        x = w_ref[:, rows, :].reshape(NI, LANES)
        o_ref[:, rows, :] = (y + x * skip).reshape(NTILE, IL, LANES)
        return c

    lax.fori_loop(0, N1, last, 0, unroll=8)


def _long_conv(wp, kf, skip, g3, h1, f2, f2i):
    nb = wp.shape[0]
    blk = pl.BlockSpec((None, NTILE, None, TILE, LANES), lambda c, b: (b, 0, c, 0, 0))
    return pl.pallas_call(
        _conv_kernel,
        out_shape=jax.ShapeDtypeStruct(wp.shape, f32),
        grid=(NSLAB, nb),
        in_specs=[
            blk,
            pl.BlockSpec((None, KPAIRS * 2 * N1, 2 * LANES), lambda c, b: (c, 0, 0), pipeline_mode=pl.Buffered(1)),
            pl.BlockSpec((None, 1, LANES), lambda c, b: (c, 0, 0)),
            _rep(g3.shape),
            _rep(h1.shape),
            _rep(f2.shape),
            _rep(f2i.shape),
        ],
        out_specs=blk,
        scratch_shapes=[pltpu.VMEM((N1 * PITCH, LANES), f32)],
        compiler_params=pltpu.CompilerParams(vmem_limit_bytes=VMEM_LIMIT),
        name="hyena_long_conv",
    )(wp, kf, skip, g3, h1, f2, f2i)


def _rms(v, g):
    return v * lax.rsqrt(jnp.mean(v * v, axis=-1, keepdims=True) + EPS) * g


def _kv_kernel(mem_ref, g_ref, wkt_ref, wv_ref, kbd_ref, vbd_ref):
    mn = _rms(mem_ref[...], g_ref[...]).astype(bf16)
    kt = lax.dot_general(wkt_ref[...], mn, (((1,), (1,)), ((), ())), preferred_element_type=f32)
    kt = kt * (HEAD_DIM ** -0.5)
    v = jnp.dot(mn, wv_ref[...], preferred_element_type=f32)
    kb = jnp.concatenate([kt] * N_XHEADS, axis=1)
    hd_bits, mem_bits = HEAD_DIM.bit_length() - 1, N_MEM.bit_length() - 1
    r = lax.broadcasted_iota(jnp.int32, kb.shape, 0) >> hd_bits
    c = lax.broadcasted_iota(jnp.int32, kb.shape, 1) >> mem_bits
    kbd_ref[...] = jnp.where(r == c, kb, 0.0).astype(bf16)
    vb = jnp.concatenate([v] * N_XHEADS, axis=0)
    r = lax.broadcasted_iota(jnp.int32, vb.shape, 0) >> mem_bits
    c = lax.broadcasted_iota(jnp.int32, vb.shape, 1) >> hd_bits
    vbd_ref[...] = jnp.where(r == c, vb, 0.0).astype(bf16)


def _memory_kv(mem, g, wkt, wv):
    nb = mem.shape[0]
    return pl.pallas_call(
        _kv_kernel,
        out_shape=(jax.ShapeDtypeStruct((nb, D_XATTN, N_XHEADS * N_MEM), bf16),
                   jax.ShapeDtypeStruct((nb, N_XHEADS * N_MEM, D_XATTN), bf16)),
        grid=(nb,),
        in_specs=[pl.BlockSpec((None, N_MEM, D_MODEL), lambda b: (b, 0, 0)), _rep(g.shape), _rep(wkt.shape),
                  _rep(wv.shape)],
        out_specs=(pl.BlockSpec((None, D_XATTN, N_XHEADS * N_MEM), lambda b: (b, 0, 0)),
                   pl.BlockSpec((None, N_XHEADS * N_MEM, D_XATTN), lambda b: (b, 0, 0))),
        name="memory_kv",
    )(mem, g, wkt, wv)


def _silu(v):
    return v * jax.nn.sigmoid(v)


def _mix_kernel(x_ref, xp_ref, xn_ref, ng_ref, win_ref, sw_ref, sb_ref, dw_ref, db_ref, lng_ref, lnb_ref,
                kbd_ref, vbd_ref, w_ref, g_ref, y_ref, hext, pbuf, glu):
    t = pl.program_id(1)
    ng = ng_ref[...]
    hext[0:HALO, :] = _rms(xp_ref[...] * (t > 0).astype(f32), ng).astype(bf16)
    hext[HALO + TILE:EXT, :] = _rms(xn_ref[...] * (t < NTILE - 1).astype(f32), ng).astype(bf16)

    def project(b0, b1):
        for m0 in range(max(b0, HALO), min(b1, HALO + TILE), N1):
            hext[m0:m0 + N1, :] = _rms(x_ref[m0 - HALO:m0 - HALO + N1, :], ng).astype(bf16)
        for c0 in range(0, D_IN, 256):
            pbuf[b0:b1, c0:c0 + 256] = jnp.dot(hext[b0:b1, :], win_ref[:, c0:c0 + 256], preferred_element_type=f32)
        glu[b0:b1, :] = (pbuf[b0:b1, C_CA:C_CA + D_CONF]
                         * jax.nn.sigmoid(pbuf[b0:b1, C_CA + D_CONF:C_CA + 2 * D_CONF]))

    kbd = kbd_ref[...]
    vbd = vbd_ref[...]
    bounds = [0, HALO + 2 * N1, HALO + 4 * N1, HALO + 6 * N1, EXT]
    done = 0
    for r in range(IL):
        while bounds[done] < N1 * r + N1 + 2 * HALO:
            project(bounds[done], bounds[done + 1])
            done += 1
        e0 = HALO + N1 * r
        rows = slice(e0, e0 + N1)
        orow = slice(N1 * r, N1 * (r + 1))

        def sconv(c0):
            acc = sb_ref[:, c0:c0 + D_HYENA]
            for k in range(SHORT_K):
                acc = acc + pbuf[e0 - 1 + k:e0 - 1 + k + N1, c0:c0 + D_HYENA] * sw_ref[k:k + 1, c0:c0 + D_HYENA]
            return acc

        x0 = sconv(0)
        wv = sconv(2 * D_HYENA) * sconv(D_HYENA)
        for s in range(NSLAB):
            w_ref[s, pl.ds(r, N1, stride=IL), :] = wv[:, LANES * s:LANES * (s + 1)]
        g_ref[orow, :] = x0 * _silu(pbuf[rows, C_HZ:C_HZ + D_HYENA])

        base = e0 - HALO
        acc = db_ref[...]
        for s in range(8):
            part = None
            for q in range((CONF_K + 8) // 8):
                k = 8 * q + s - 1
                if 0 <= k < CONF_K:
                    term = glu[base + 8 * q:base + 8 * q + N1 + 8, :] * dw_ref[k:k + 1, :]
                    part = term if part is None else part + term
            acc = acc + part[s:s + N1, :]
        mu = jnp.mean(acc, axis=-1, keepdims=True)
        xc = acc - mu
        ln = xc * lax.rsqrt(jnp.mean(xc * xc, axis=-1, keepdims=True) + EPS) * lng_ref[...] + lnb_ref[...]
        y_ref[orow, 0:D_CONF] = (_silu(ln) * _silu(pbuf[rows, C_CZ:C_CZ + D_CONF])).astype(bf16)

        q = pbuf[rows, C_XQ:C_XQ + D_XATTN].astype(bf16)
        s_all = jnp.dot(q, kbd, preferred_element_type=f32)
        ps = []
        for h in range(N_XHEADS):
            sh = s_all[:, N_MEM * h:N_MEM * (h + 1)]
            e = jnp.exp(sh - jnp.max(sh, axis=-1, keepdims=True))
            ps.append((e / jnp.sum(e, axis=-1, keepdims=True)).astype(bf16))
        o = jnp.dot(jnp.concatenate(ps, axis=1), vbd, preferred_element_type=f32)
        y_ref[orow, D_CONF:D_CONF + D_XATTN] = (o * _silu(pbuf[rows, C_XZ:C_XZ + D_XATTN])).astype(bf16)


def _mix(x, ng, win, sw, sb, dw, db, lng, lnb, kbd, vbd):
    nb = x.shape[0]
    hb = TILE // HALO
    nhb = SEQ // HALO
    small = [ng, win, sw, sb, dw, db, lng, lnb]
    return pl.pallas_call(
        _mix_kernel,
        out_shape=(jax.ShapeDtypeStruct((nb, NTILE, NSLAB, TILE, LANES), f32),
                   jax.ShapeDtypeStruct((nb, SEQ, D_HYENA), f32),
                   jax.ShapeDtypeStruct((nb, SEQ, D_CONF + D_XATTN), bf16)),
        grid=(nb, NTILE),
        in_specs=[
            pl.BlockSpec((None, TILE, D_MODEL), lambda b, t: (b, t, 0)),
            pl.BlockSpec((None, HALO, D_MODEL), lambda b, t: (b, jnp.maximum(t * hb - 1, 0), 0)),
            pl.BlockSpec((None, HALO, D_MODEL), lambda b, t: (b, jnp.minimum((t + 1) * hb, nhb - 1), 0)),
        ] + [_rep(a.shape) for a in small] + [
            pl.BlockSpec((None, D_XATTN, N_XHEADS * N_MEM), lambda b, t: (b, 0, 0)),
            pl.BlockSpec((None, N_XHEADS * N_MEM, D_XATTN), lambda b, t: (b, 0, 0)),
        ],
        out_specs=(pl.BlockSpec((None, None, NSLAB, TILE, LANES), lambda b, t: (b, t, 0, 0, 0)),
                   pl.BlockSpec((None, TILE, D_HYENA), lambda b, t: (b, t, 0)),
                   pl.BlockSpec((None, TILE, D_CONF + D_XATTN), lambda b, t: (b, t, 0))),
        scratch_shapes=[pltpu.VMEM((EXT, D_MODEL), bf16), pltpu.VMEM((EXT, D_IN), f32),
                        pltpu.VMEM((EXT, D_CONF), f32)],
        compiler_params=pltpu.CompilerParams(vmem_limit_bytes=VMEM_LIMIT),
        name="token_mix",
    )(x, x, x, *small, kbd, vbd)


def _out_kernel(x_ref, g_ref, c_ref, y_ref, wout_ref, fg_ref, o_ref, mix, *, final):
    for r in range(IL):
        rows = slice(N1 * r, N1 * (r + 1))
        for s in range(NSLAB):
            cols = slice(LANES * s, LANES * (s + 1))
            conv = c_ref[s, pl.ds(r, N1, stride=IL), :]
            mix[rows, cols] = (g_ref[rows, cols] * conv).astype(bf16)
    mix[:, D_HYENA:] = y_ref[...]
    xn = x_ref[...] + jnp.dot(mix[...], wout_ref[...], preferred_element_type=f32)
    o_ref[...] = _rms(xn, fg_ref[...]) if final else xn


def _out_proj(x, g, cp, y, wout, fg, final):
    nb = x.shape[0]
    return pl.pallas_call(
        functools.partial(_out_kernel, final=final),
        out_shape=jax.ShapeDtypeStruct(x.shape, f32),
        grid=(nb, NTILE),
        in_specs=[
            pl.BlockSpec((None, TILE, D_MODEL), lambda b, t: (b, t, 0)),
            pl.BlockSpec((None, TILE, D_HYENA), lambda b, t: (b, t, 0)),
            pl.BlockSpec((None, None, NSLAB, TILE, LANES), lambda b, t: (b, t, 0, 0, 0)),
            pl.BlockSpec((None, TILE, D_CONF + D_XATTN), lambda b, t: (b, t, 0)),
            _rep(wout.shape),
            _rep(fg.shape),
        ],
        out_specs=pl.BlockSpec((None, TILE, D_MODEL), lambda b, t: (b, t, 0)),
        scratch_shapes=[pltpu.VMEM((TILE, D_MODEL), bf16)],
        compiler_params=pltpu.CompilerParams(vmem_limit_bytes=VMEM_LIMIT),
        name="out_proj",
    )(x, g, cp, y, wout, fg)


def kernel(x_prompt, x_sample, mem_prompt, mem_sample, norm_g, mem_norm_g, w_in, hy_short_w, hy_short_b, hy_f_w1, hy_f_b1, hy_f_fr1, hy_f_w2, hy_f_b2, hy_f_fr2, hy_f_w3, hy_f_b3, hy_f_fr3, hy_f_w4, hy_skip, cf_dw_w, cf_dw_b, cf_ln_g, cf_ln_b, xa_w_kv, w_out, final_g):
    nbp = x_prompt.shape[0]
    x = jnp.concatenate([x_prompt, x_sample], axis=0)
    mem = jnp.concatenate([mem_prompt, mem_sample], axis=0)
    g3, h1, f2x3, f2, f2i = _dft_tables()
    z, deltas = _positional()
    row = lambda v: v[None, :]
    for l in range(DEPTH):
        h = _filter(z, deltas, hy_f_w1[l], hy_f_b1[l], hy_f_fr1[l], hy_f_w2[l], hy_f_b2[l], hy_f_fr2[l],
                    hy_f_w3[l], hy_f_b3[l], hy_f_fr3[l], hy_f_w4[l])
        kf = _spectrum(h, g3, f2x3)
        kbd, vbd = _memory_kv(mem, row(mem_norm_g[l]), xa_w_kv[l][:, :D_XATTN].T.astype(bf16),
                              xa_w_kv[l][:, D_XATTN:].astype(bf16))
        wp, g, y = _mix(x, row(norm_g[l]), w_in[l].astype(bf16), hy_short_w[l], row(hy_short_b[l]), cf_dw_w[l],
                        row(cf_dw_b[l]), row(cf_ln_g[l]), row(cf_ln_b[l]), kbd, vbd)
        cp = _long_conv(wp, kf, hy_skip[l].reshape(NSLAB, 1, LANES), g3, h1, f2, f2i)
        x = _out_proj(x, g, cp, y, w_out[l].astype(bf16), row(final_g), final=(l == DEPTH - 1))
    return x[:nbp], x[nbp:]
```

```python
import functools
import math

import jax
import jax.numpy as jnp
from jax import lax
from jax.experimental import pallas as pl
from jax.experimental.pallas import tpu as pltpu

f32 = jnp.float32
bf16 = jnp.bfloat16

D_MODEL = 1024
SEQ = 8192
DEPTH = 4
N_MEM = 256
HEAD_DIM = 64
D_HYENA = 512
D_CONF = 256
D_XATTN = 256
N_XHEADS = D_XATTN // HEAD_DIM
SHORT_K = 3
CONF_K = 31
FILT_BANDS = 16
FILT_EMB = 1 + 2 * FILT_BANDS
FILT_HIDDEN = 64
DECAY_TARGET = 1e-2
FAST_DECAY_PCT = 0.3
SLOW_DECAY_PCT = 1.5
EPS = 1e-6
P_HY_U = 3 * D_HYENA
D_IN = P_HY_U + D_HYENA + 2 * D_CONF + D_CONF + D_XATTN + D_XATTN
C_HZ = P_HY_U
C_CA = C_HZ + D_HYENA
C_CZ = C_CA + 2 * D_CONF
C_XQ = C_CZ + D_CONF
C_XZ = C_XQ + D_XATTN

LANES = 128
N1 = 128
NFFT = N1 * N1
NI = SEQ // N1
K1 = N1 // 2 + 1
K1P = 72
KPAIRS = (K1 + 1) // 2
SROWS = 2 * K1P
PITCH = 152
PROJ_COLS = 256
EDGE_UNROLL = 16
MID_UNROLL = 11
TILE = 1024
NTILE = SEQ // TILE
IL = TILE // N1
NSLAB = D_HYENA // LANES
HALO = 16
EXT = TILE + 2 * HALO
VMEM_LIMIT = 60 * 1024 * 1024


def _split(x):
    hi = x.astype(bf16)
    lo = (x - hi.astype(f32)).astype(bf16)
    return hi, lo


def _rep(shape):
    return pl.BlockSpec(shape, lambda *_: (0,) * len(shape), pipeline_mode=pl.Buffered(1))


def _dft_tables():
    k1 = jnp.arange(K1P, dtype=jnp.int32)
    i = jnp.arange(NI, dtype=jnp.int32)
    j = jnp.arange(N1, dtype=jnp.int32)
    valid = (k1 < K1).astype(f32)
    scale = 2.0 * math.pi / NFFT
    ph = (j[:, None, None] * k1[None, :, None] + N1 * k1[None, :, None] * i[None, None, :]) % NFFT
    ang = ph.astype(f32) * scale
    g = jnp.concatenate([jnp.cos(ang) * valid[None, :, None], -jnp.sin(ang) * valid[None, :, None]], axis=1)
    g_hi, g_lo = _split(g)
    g3 = jnp.concatenate([g_hi, g_hi, g_lo, jnp.zeros((N1, SROWS, 4 * NI - 3 * NI), bf16)], axis=2)
    cw = jnp.where((k1 == 0) | (k1 == N1 // 2), 1.0, 2.0) * valid / NFFT
    ph4 = (N1 * i[None, :, None] * k1[None, None, :] + j[:, None, None] * k1[None, None, :]) % NFFT
    ang4 = ph4.astype(f32) * scale
    h = jnp.concatenate([jnp.cos(ang4) * cw, -jnp.sin(ang4) * cw], axis=2)
    h1 = jnp.concatenate([h.astype(bf16), jnp.zeros((N1, NI, 2 * N1 - SROWS), bf16)], axis=2)
    th = ((j[:, None] * j[None, :]) % N1).astype(f32) * (2.0 * math.pi / N1)
    c, s = jnp.cos(th), jnp.sin(th)
    fwd = jnp.concatenate([jnp.concatenate([c, s], 1), jnp.concatenate([-s, c], 1)], 0)
    inv = jnp.concatenate([jnp.concatenate([c, -s], 1), jnp.concatenate([s, c], 1)], 0)

    def cat3(m):
        m_hi, m_lo = _split(m)
        return jnp.concatenate([m_hi, m_hi, m_lo], axis=1)

    return g3, h1, cat3(fwd), fwd.astype(bf16), inv.astype(bf16)


def _positional():
    t = jnp.linspace(0.0, 1.0, SEQ, dtype=f32)[:, None]
    n = jnp.arange(SEQ, dtype=f32)[:, None]
    bands = jnp.linspace(1e-4, FILT_BANDS - 1, FILT_BANDS, dtype=f32)[None, :]
    ang = (2.0 * math.pi / SEQ) * bands * n
    z = jnp.concatenate([t, jnp.cos(ang), -jnp.sin(ang)], axis=-1)
    z = jnp.pad(z, ((0, 0), (0, LANES - FILT_EMB)))
    z = z.reshape(NTILE, IL, N1, LANES).transpose(0, 2, 1, 3).reshape(NTILE, TILE, LANES)
    max_decay = math.log(DECAY_TARGET) / FAST_DECAY_PCT
    min_decay = math.log(DECAY_TARGET) / SLOW_DECAY_PCT
    deltas = jnp.abs(jnp.linspace(min_decay, max_decay, D_HYENA, dtype=f32))
    return z, jnp.concatenate([deltas, deltas])[None, :]


def _filter_kernel(z_ref, w1_ref, b1_ref, fr1_ref, w2_ref, b2_ref, fr2_ref, w3_ref, b3_ref, fr3_ref,
                   w4_ref, dl_ref, o_ref):
    hp = lax.Precision.HIGHEST
    z = z_ref[...]
    t = z[:, 0:1]
    h = jnp.sin(fr1_ref[...] * (jnp.dot(z, w1_ref[...], precision=hp, preferred_element_type=f32) + b1_ref[...]))
    h = jnp.sin(fr2_ref[...] * (jnp.dot(h, w2_ref[...], precision=hp, preferred_element_type=f32) + b2_ref[...]))
    h = jnp.sin(fr3_ref[...] * (jnp.dot(h, w3_ref[...], precision=hp, preferred_element_type=f32) + b3_ref[...]))
    h = jnp.dot(h, w4_ref[...], precision=hp, preferred_element_type=f32)
    h = h * jnp.exp(-t * dl_ref[...])
    col = lax.broadcasted_iota(jnp.int32, h.shape, 1)
    o_ref[...] = jnp.where((col >= D_HYENA) & (t <= 0.0), 0.0, h)


def _filter(z, deltas, w1, b1, fr1, w2, b2, fr2, w3, b3, fr3, w4):
    hp = LANES - FILT_HIDDEN
    w1p = jnp.pad(w1, ((0, LANES - FILT_EMB), (0, hp)))
    w2p = jnp.pad(w2, ((0, hp), (0, hp)))
    w3p = jnp.pad(w3, ((0, hp), (0, hp)))
    w4p = jnp.pad(w4, ((0, hp), (0, 0)))
    vec = lambda v: jnp.pad(v, (0, hp))[None, :]
    args = (z, w1p, vec(b1), vec(fr1), w2p, vec(b2), vec(fr2), w3p, vec(b3), vec(fr3), w4p, deltas)
    specs = [pl.BlockSpec((None, TILE, LANES), lambda t: (t, 0, 0))] + [_rep(a.shape) for a in args[1:]]
    return pl.pallas_call(
        _filter_kernel,
        out_shape=jax.ShapeDtypeStruct((NTILE, TILE, 2 * D_HYENA), f32),
        grid=(NTILE,),
        in_specs=specs,
        out_specs=pl.BlockSpec((None, TILE, 2 * D_HYENA), lambda t: (t, 0, 0)),
        name="hyena_filter",
    )(*args)


def _stage1(src_ref, g3_ref, s_ref):
    pad = jnp.zeros((NI, LANES), bf16)

    def body(j, c):
        x = src_ref[:, pl.ds(pl.multiple_of(j * IL, IL), IL), :].reshape(NI, LANES)
        hi, lo = _split(x)
        rhs = jnp.concatenate([hi, lo, hi, pad], axis=0)
        a = jnp.dot(g3_ref[j], rhs, preferred_element_type=f32)
        s_ref[pl.ds(pl.multiple_of(j * PITCH, 8), SROWS), :] = a
        return c

    lax.fori_loop(0, N1, body, 0, unroll=EDGE_UNROLL)


def _dot3(f_ref, a):
    hi, lo = _split(a)
    return jnp.dot(f_ref[...], jnp.concatenate([hi, lo, hi], axis=0), preferred_element_type=f32)


def _load_pair(s_ref, kk):
    k = 2 * kk
    parts = [jnp.concatenate([s_ref[pl.ds(off + k, N1, stride=PITCH), :],
                              s_ref[pl.ds(off + k + 1, N1, stride=PITCH), :]], axis=1) for off in (0, K1P)]
    return jnp.concatenate(parts, axis=0)


def _pair_rows(kk):
    return pl.ds(pl.multiple_of(kk * 2 * N1, 2 * N1), 2 * N1)


def _spectrum_kernel(hf_ref, hb_ref, g3_ref, f2_ref, o_ref, s_ref):
    _stage1(hf_ref, g3_ref, s_ref)

    def fwd(kk, c):
        o_ref[_pair_rows(kk), :] = _dot3(f2_ref, _load_pair(s_ref, kk))
        return c

    lax.fori_loop(0, KPAIRS, fwd, 0, unroll=3)
    _stage1(hb_ref, g3_ref, s_ref)

    def bwd(kk, c):
        a = _dot3(f2_ref, _load_pair(s_ref, kk))
        r0 = pl.multiple_of(kk * 2 * N1, 2 * N1)
        o_ref[pl.ds(r0, N1), :] = o_ref[pl.ds(r0, N1), :] + a[:N1]
        o_ref[pl.ds(r0 + N1, N1), :] = o_ref[pl.ds(r0 + N1, N1), :] - a[N1:]
        return c

    lax.fori_loop(0, KPAIRS, bwd, 0, unroll=3)


def _spectrum(h, g3, f2):
    return pl.pallas_call(
        _spectrum_kernel,
        out_shape=jax.ShapeDtypeStruct((NSLAB, KPAIRS * 2 * N1, 2 * LANES), f32),
        grid=(NSLAB,),
        in_specs=[
            pl.BlockSpec((NTILE, TILE, LANES), lambda c: (0, 0, c)),
            pl.BlockSpec((NTILE, TILE, LANES), lambda c: (0, 0, NSLAB + c)),
            _rep(g3.shape),
            _rep(f2.shape),
        ],
        out_specs=pl.BlockSpec((None, KPAIRS * 2 * N1, 2 * LANES), lambda c: (c, 0, 0)),
        scratch_shapes=[pltpu.VMEM((N1 * PITCH, LANES), f32)],
        compiler_params=pltpu.CompilerParams(vmem_limit_bytes=VMEM_LIMIT),
        name="hyena_spectrum",
    )(h, h, g3, f2)


def _conv_kernel(w_ref, kf_ref, skip_ref, g3_ref, h1_ref, f2_ref, f2i_ref, o_ref, s_ref):
    _stage1(w_ref, g3_ref, s_ref)

    def mid(kk, c):
        a = jnp.dot(f2_ref[...], _load_pair(s_ref, kk).astype(bf16), preferred_element_type=f32)
        kf = kf_ref[_pair_rows(kk), :]
        xr, xi, kr, ki = a[:N1], a[N1:], kf[:N1], kf[N1:]
        y = jnp.concatenate([xr * kr - xi * ki, xr * ki + xi * kr], axis=0)
        b = jnp.dot(f2i_ref[...], y.astype(bf16), preferred_element_type=f32)
        k = 2 * kk
        for p in range(2):
            s_ref[pl.ds(k + p, N1, stride=PITCH), :] = b[:N1, LANES * p:LANES * (p + 1)]
            s_ref[pl.ds(K1P + k + p, N1, stride=PITCH), :] = b[N1:, LANES * p:LANES * (p + 1)]
        return c

    lax.fori_loop(0, KPAIRS, mid, 0, unroll=MID_UNROLL)
    pad = jnp.zeros((2 * N1 - SROWS, LANES), bf16)
    skip = skip_ref[...]

    def last(j, c):
        b = s_ref[pl.ds(pl.multiple_of(j * PITCH, 8), SROWS), :].astype(bf16)
        y = jnp.dot(h1_ref[j], jnp.concatenate([b, pad], axis=0), preferred_element_type=f32)
        rows = pl.ds(pl.multiple_of(j * IL, IL), IL)
        x = w_ref[:, rows, :].reshape(NI, LANES)
        o_ref[:, rows, :] = (y + x * skip).reshape(NTILE, IL, LANES)
        return c

    lax.fori_loop(0, N1, last, 0, unroll=EDGE_UNROLL)


def _long_conv(wp, kf, skip, g3, h1, f2, f2i):
    nb = wp.shape[0]
    blk = pl.BlockSpec((None, NTILE, None, TILE, LANES), lambda c, b: (b, 0, c, 0, 0))
    return pl.pallas_call(
        _conv_kernel,
        out_shape=jax.ShapeDtypeStruct(wp.shape, f32),
        grid=(NSLAB, nb),
        in_specs=[
            blk,
            pl.BlockSpec((None, KPAIRS * 2 * N1, 2 * LANES), lambda c, b: (c, 0, 0), pipeline_mode=pl.Buffered(1)),
            pl.BlockSpec((None, 1, LANES), lambda c, b: (c, 0, 0)),
            _rep(g3.shape),
            _rep(h1.shape),
            _rep(f2.shape),
            _rep(f2i.shape),
        ],
        out_specs=blk,
        scratch_shapes=[pltpu.VMEM((N1 * PITCH, LANES), f32)],
        compiler_params=pltpu.CompilerParams(vmem_limit_bytes=VMEM_LIMIT),
        name="hyena_long_conv",
    )(wp, kf, skip, g3, h1, f2, f2i)


def _rms(v, g):
    return v * lax.rsqrt(jnp.mean(v * v, axis=-1, keepdims=True) + EPS) * g


def _kv_kernel(mem_ref, g_ref, wkt_ref, wv_ref, kbd_ref, vbd_ref):
    mn = _rms(mem_ref[...], g_ref[...]).astype(bf16)
    kt = lax.dot_general(wkt_ref[...], mn, (((1,), (1,)), ((), ())), preferred_element_type=f32)
    kt = kt * (HEAD_DIM ** -0.5)
    v = jnp.dot(mn, wv_ref[...], preferred_element_type=f32)
    kb = jnp.concatenate([kt] * N_XHEADS, axis=1)
    hd_bits, mem_bits = HEAD_DIM.bit_length() - 1, N_MEM.bit_length() - 1
    r = lax.broadcasted_iota(jnp.int32, kb.shape, 0) >> hd_bits
    c = lax.broadcasted_iota(jnp.int32, kb.shape, 1) >> mem_bits
    kbd_ref[...] = jnp.where(r == c, kb, 0.0).astype(bf16)
    vb = jnp.concatenate([v] * N_XHEADS, axis=0)
    r = lax.broadcasted_iota(jnp.int32, vb.shape, 0) >> mem_bits
    c = lax.broadcasted_iota(jnp.int32, vb.shape, 1) >> hd_bits
    vbd_ref[...] = jnp.where(r == c, vb, 0.0).astype(bf16)


def _memory_kv(mem, g, wkt, wv):
    nb = mem.shape[0]
    return pl.pallas_call(
        _kv_kernel,
        out_shape=(jax.ShapeDtypeStruct((nb, D_XATTN, N_XHEADS * N_MEM), bf16),
                   jax.ShapeDtypeStruct((nb, N_XHEADS * N_MEM, D_XATTN), bf16)),
        grid=(nb,),
        in_specs=[pl.BlockSpec((None, N_MEM, D_MODEL), lambda b: (b, 0, 0)), _rep(g.shape), _rep(wkt.shape),
                  _rep(wv.shape)],
        out_specs=(pl.BlockSpec((None, D_XATTN, N_XHEADS * N_MEM), lambda b: (b, 0, 0)),
                   pl.BlockSpec((None, N_XHEADS * N_MEM, D_XATTN), lambda b: (b, 0, 0))),
        name="memory_kv",
    )(mem, g, wkt, wv)


def _silu(v):
    return v * jax.nn.sigmoid(v)


def _mix_kernel(x_ref, xp_ref, xn_ref, ng_ref, win_ref, sw_ref, sb_ref, dw_ref, db_ref, lng_ref, lnb_ref,
                kbd_ref, vbd_ref, w_ref, g_ref, y_ref, hext, pbuf, glu):
    t = pl.program_id(1)
    ng = ng_ref[...]
    hext[0:HALO, :] = _rms(xp_ref[...] * (t > 0).astype(f32), ng).astype(bf16)
    hext[HALO + TILE:EXT, :] = _rms(xn_ref[...] * (t < NTILE - 1).astype(f32), ng).astype(bf16)

    def project(b0, b1):
        m0, m1 = max(b0, HALO), min(b1, HALO + TILE)
        hext[m0:m1, :] = _rms(x_ref[m0 - HALO:m1 - HALO, :], ng).astype(bf16)
        for c0 in range(0, D_IN, PROJ_COLS):
            width = min(PROJ_COLS, D_IN - c0)
            res = jnp.dot(hext[b0:b1, :], win_ref[:, c0:c0 + width], preferred_element_type=f32)
            for i in range(width // LANES):
                pbuf[c0 // LANES + i, b0:b1, :] = res[:, LANES * i:LANES * (i + 1)]
        for c in range(D_CONF // LANES):
            glu[c, b0:b1, :] = (pbuf[C_CA // LANES + c, b0:b1, :]
                                * jax.nn.sigmoid(pbuf[(C_CA + D_CONF) // LANES + c, b0:b1, :]))

    def cols(c0, width, rows):
        return jnp.concatenate([pbuf[c0 // LANES + i, rows, :] for i in range(width // LANES)], axis=1)

    kbd = kbd_ref[...]
    vbd = vbd_ref[...]
    bounds = [0] + [2 * HALO + 2 * N1 * b for b in range(1, IL // 2)] + [EXT]
    done = 0
    for r in range(IL):
        while bounds[done] < N1 * r + N1 + 2 * HALO:
            project(bounds[done], bounds[done + 1])
            done += 1
        e0 = HALO + N1 * r
        rows = slice(e0, e0 + N1)
        orow = slice(N1 * r, N1 * (r + 1))

        def sconv(slab):
            lanes = slice(LANES * slab, LANES * (slab + 1))
            acc = sb_ref[:, lanes]
            for k in range(SHORT_K):
                acc = acc + pbuf[slab, e0 - 1 + k:e0 - 1 + k + N1, :] * sw_ref[k:k + 1, lanes]
            return acc

        for s in range(NSLAB):
            wv = sconv(2 * NSLAB + s) * sconv(NSLAB + s)
            w_ref[s, pl.ds(r, N1, stride=IL), :] = wv
            g_ref[orow, LANES * s:LANES * (s + 1)] = sconv(s) * _silu(pbuf[C_HZ // LANES + s, rows, :])

        accs = []
        for c in range(D_CONF // LANES):
            lanes = slice(LANES * c, LANES * (c + 1))
            acc = db_ref[:, lanes]
            for k in range(CONF_K):
                lo = e0 - CONF_K // 2 + k
                acc = acc + glu[c, lo:lo + N1, :] * dw_ref[k:k + 1, lanes]
            accs.append(acc)
        acc = jnp.concatenate(accs, axis=1)
        mu = jnp.mean(acc, axis=-1, keepdims=True)
        xc = acc - mu
        ln = xc * lax.rsqrt(jnp.mean(xc * xc, axis=-1, keepdims=True) + EPS) * lng_ref[...] + lnb_ref[...]
        y_ref[orow, 0:D_CONF] = (_silu(ln) * _silu(cols(C_CZ, D_CONF, rows))).astype(bf16)

        q = cols(C_XQ, D_XATTN, rows).astype(bf16)
        s_all = jnp.dot(q, kbd, preferred_element_type=f32)
        ps = []
        for h in range(N_XHEADS):
            sh = s_all[:, N_MEM * h:N_MEM * (h + 1)]
            e = jnp.exp(sh - jnp.max(sh, axis=-1, keepdims=True))
            ps.append((e / jnp.sum(e, axis=-1, keepdims=True)).astype(bf16))
        o = jnp.dot(jnp.concatenate(ps, axis=1), vbd, preferred_element_type=f32)
        y_ref[orow, D_CONF:D_CONF + D_XATTN] = (o * _silu(cols(C_XZ, D_XATTN, rows))).astype(bf16)


def _mix(x, ng, win, sw, sb, dw, db, lng, lnb, kbd, vbd):
    nb = x.shape[0]
    hb = TILE // HALO
    nhb = SEQ // HALO
    small = [ng, win, sw, sb, dw, db, lng, lnb]
    return pl.pallas_call(
        _mix_kernel,
        out_shape=(jax.ShapeDtypeStruct((nb, NTILE, NSLAB, TILE, LANES), f32),
                   jax.ShapeDtypeStruct((nb, SEQ, D_HYENA), f32),
                   jax.ShapeDtypeStruct((nb, SEQ, D_CONF + D_XATTN), bf16)),
        grid=(nb, NTILE),
        in_specs=[
            pl.BlockSpec((None, TILE, D_MODEL), lambda b, t: (b, t, 0)),
            pl.BlockSpec((None, HALO, D_MODEL), lambda b, t: (b, jnp.maximum(t * hb - 1, 0), 0)),
            pl.BlockSpec((None, HALO, D_MODEL), lambda b, t: (b, jnp.minimum((t + 1) * hb, nhb - 1), 0)),
        ] + [_rep(a.shape) for a in small] + [
            pl.BlockSpec((None, D_XATTN, N_XHEADS * N_MEM), lambda b, t: (b, 0, 0)),
            pl.BlockSpec((None, N_XHEADS * N_MEM, D_XATTN), lambda b, t: (b, 0, 0)),
        ],
        out_specs=(pl.BlockSpec((None, None, NSLAB, TILE, LANES), lambda b, t: (b, t, 0, 0, 0)),
                   pl.BlockSpec((None, TILE, D_HYENA), lambda b, t: (b, t, 0)),
                   pl.BlockSpec((None, TILE, D_CONF + D_XATTN), lambda b, t: (b, t, 0))),
        scratch_shapes=[pltpu.VMEM((EXT, D_MODEL), bf16), pltpu.VMEM((D_IN // LANES, EXT, LANES), f32),
                        pltpu.VMEM((D_CONF // LANES, EXT, LANES), f32)],
        compiler_params=pltpu.CompilerParams(vmem_limit_bytes=VMEM_LIMIT),
        name="token_mix",
    )(x, x, x, *small, kbd, vbd)


def _out_kernel(x_ref, g_ref, c_ref, y_ref, wout_ref, fg_ref, o_ref, mix, *, final):
    for r in range(IL):
        rows = slice(N1 * r, N1 * (r + 1))
        for s in range(NSLAB):
            cols = slice(LANES * s, LANES * (s + 1))
            conv = c_ref[s, pl.ds(r, N1, stride=IL), :]
            mix[rows, cols] = (g_ref[rows, cols] * conv).astype(bf16)
    mix[:, D_HYENA:] = y_ref[...]
    xn = x_ref[...] + jnp.dot(mix[...], wout_ref[...], preferred_element_type=f32)
    o_ref[...] = _rms(xn, fg_ref[...]) if final else xn


def _out_proj(x, g, cp, y, wout, fg, final):
    nb = x.shape[0]
    return pl.pallas_call(
        functools.partial(_out_kernel, final=final),
        out_shape=jax.ShapeDtypeStruct(x.shape, f32),
        grid=(nb, NTILE),
        in_specs=[
            pl.BlockSpec((None, TILE, D_MODEL), lambda b, t: (b, t, 0)),
            pl.BlockSpec((None, TILE, D_HYENA), lambda b, t: (b, t, 0)),
            pl.BlockSpec((None, None, NSLAB, TILE, LANES), lambda b, t: (b, t, 0, 0, 0)),
            pl.BlockSpec((None, TILE, D_CONF + D_XATTN), lambda b, t: (b, t, 0)),
            _rep(wout.shape),
            _rep(fg.shape),
        ],
        out_specs=pl.BlockSpec((None, TILE, D_MODEL), lambda b, t: (b, t, 0)),
        scratch_shapes=[pltpu.VMEM((TILE, D_MODEL), bf16)],
        compiler_params=pltpu.CompilerParams(vmem_limit_bytes=VMEM_LIMIT),
        name="out_proj",
    )(x, g, cp, y, wout, fg)


def kernel(x_prompt, x_sample, mem_prompt, mem_sample, norm_g, mem_norm_g, w_in, hy_short_w, hy_short_b, hy_f_w1, hy_f_b1, hy_f_fr1, hy_f_w2, hy_f_b2, hy_f_fr2, hy_f_w3, hy_f_b3, hy_f_fr3, hy_f_w4, hy_skip, cf_dw_w, cf_dw_b, cf_ln_g, cf_ln_b, xa_w_kv, w_out, final_g):
    g3, h1, f2x3, f2, f2i = _dft_tables()
    z, deltas = _positional()
    row = lambda v: v[None, :]
    xs = [x_prompt, x_sample]
    mems = [mem_prompt, mem_sample]
    for l in range(DEPTH):
        h = _filter(z, deltas, hy_f_w1[l], hy_f_b1[l], hy_f_fr1[l], hy_f_w2[l], hy_f_b2[l], hy_f_fr2[l],
                    hy_f_w3[l], hy_f_b3[l], hy_f_fr3[l], hy_f_w4[l])
        kf = _spectrum(h, g3, f2x3)
        win, wout = w_in[l].astype(bf16), w_out[l].astype(bf16)
        wkt, wv = xa_w_kv[l][:, :D_XATTN].T.astype(bf16), xa_w_kv[l][:, D_XATTN:].astype(bf16)
        skip = hy_skip[l].reshape(NSLAB, 1, LANES)
        for i in range(len(xs)):
            kbd, vbd = _memory_kv(mems[i], row(mem_norm_g[l]), wkt, wv)
            wp, g, y = _mix(xs[i], row(norm_g[l]), win, hy_short_w[l], row(hy_short_b[l]), cf_dw_w[l],
                            row(cf_dw_b[l]), row(cf_ln_g[l]), row(cf_ln_b[l]), kbd, vbd)
            cp = _long_conv(wp, kf, skip, g3, h1, f2, f2i)
            xs[i] = _out_proj(xs[i], g, cp, y, wout, row(final_g), final=(l == DEPTH - 1))
    return xs[0], xs[1]
```

```python
import functools
import math

import jax
import jax.numpy as jnp
from jax import lax
from jax.experimental import pallas as pl
from jax.experimental.pallas import tpu as pltpu

f32 = jnp.float32
bf16 = jnp.bfloat16

D_MODEL = 1024
SEQ = 8192
DEPTH = 4
N_MEM = 256
HEAD_DIM = 64
D_HYENA = 512
D_CONF = 256
D_XATTN = 256
N_XHEADS = D_XATTN // HEAD_DIM
SHORT_K = 3
CONF_K = 31
FILT_BANDS = 16
FILT_EMB = 1 + 2 * FILT_BANDS
FILT_HIDDEN = 64
DECAY_TARGET = 1e-2
FAST_DECAY_PCT = 0.3
SLOW_DECAY_PCT = 1.5
EPS = 1e-6
P_HY_U = 3 * D_HYENA
D_IN = P_HY_U + D_HYENA + 2 * D_CONF + D_CONF + D_XATTN + D_XATTN
C_HZ = P_HY_U
C_CA = C_HZ + D_HYENA
C_CZ = C_CA + 2 * D_CONF
C_XQ = C_CZ + D_CONF
C_XZ = C_XQ + D_XATTN

LANES = 128
N1 = 128
NFFT = N1 * N1
NI = SEQ // N1
K1 = N1 // 2 + 1
K1P = 72
KPAIRS = (K1 + 1) // 2
SROWS = 2 * K1P
PITCH = 148
PROJ_COLS = 256
EDGE_UNROLL = 32
MID_UNROLL = 11
TILE = 1024
NTILE = SEQ // TILE
IL = TILE // N1
NSLAB = D_HYENA // LANES
HALO = 16
EXT = TILE + 2 * HALO
VMEM_LIMIT = 60 * 1024 * 1024


def _split(x):
    hi = x.astype(bf16)
    lo = (x - hi.astype(f32)).astype(bf16)
    return hi, lo


def _rep(shape):
    return pl.BlockSpec(shape, lambda *_: (0,) * len(shape), pipeline_mode=pl.Buffered(1))


def _dft_tables():
    k1 = jnp.arange(K1P, dtype=jnp.int32)
    i = jnp.arange(NI, dtype=jnp.int32)
    j = jnp.arange(N1, dtype=jnp.int32)
    valid = (k1 < K1).astype(f32)
    scale = 2.0 * math.pi / NFFT
    ph = (j[:, None, None] * k1[None, :, None] + N1 * k1[None, :, None] * i[None, None, :]) % NFFT
    ang = ph.astype(f32) * scale
    g = jnp.concatenate([jnp.cos(ang) * valid[None, :, None], -jnp.sin(ang) * valid[None, :, None]], axis=1)
    g_hi, g_lo = _split(g)
    g3 = jnp.concatenate([g_hi, g_hi, g_lo, jnp.zeros((N1, SROWS, 4 * NI - 3 * NI), bf16)], axis=2)
    cw = jnp.where((k1 == 0) | (k1 == N1 // 2), 1.0, 2.0) * valid / NFFT
    ph4 = (N1 * i[None, :, None] * k1[None, None, :] + j[:, None, None] * k1[None, None, :]) % NFFT
    ang4 = ph4.astype(f32) * scale
    h = jnp.concatenate([jnp.cos(ang4) * cw, -jnp.sin(ang4) * cw], axis=2)
    h1 = jnp.concatenate([h.astype(bf16), jnp.zeros((N1, NI, 2 * N1 - SROWS), bf16)], axis=2)
    th = ((j[:, None] * j[None, :]) % N1).astype(f32) * (2.0 * math.pi / N1)
    c, s = jnp.cos(th), jnp.sin(th)
    fwd = jnp.concatenate([jnp.concatenate([c, s], 1), jnp.concatenate([-s, c], 1)], 0)
    inv = jnp.concatenate([jnp.concatenate([c, -s], 1), jnp.concatenate([s, c], 1)], 0)

    def cat3(m):
        m_hi, m_lo = _split(m)
        return jnp.concatenate([m_hi, m_hi, m_lo], axis=1)

    return g3, h1, cat3(fwd), fwd.astype(bf16), inv.astype(bf16)


def _positional():
    t = jnp.linspace(0.0, 1.0, SEQ, dtype=f32)[:, None]
    n = jnp.arange(SEQ, dtype=f32)[:, None]
    bands = jnp.linspace(1e-4, FILT_BANDS - 1, FILT_BANDS, dtype=f32)[None, :]
    ang = (2.0 * math.pi / SEQ) * bands * n
    z = jnp.concatenate([t, jnp.cos(ang), -jnp.sin(ang)], axis=-1)
    z = jnp.pad(z, ((0, 0), (0, LANES - FILT_EMB)))
    z = z.reshape(NTILE, IL, N1, LANES).transpose(0, 2, 1, 3).reshape(NTILE, TILE, LANES)
    max_decay = math.log(DECAY_TARGET) / FAST_DECAY_PCT
    min_decay = math.log(DECAY_TARGET) / SLOW_DECAY_PCT
    deltas = jnp.abs(jnp.linspace(min_decay, max_decay, D_HYENA, dtype=f32))
    return z, jnp.concatenate([deltas, deltas])[None, :]


def _filter_kernel(z_ref, w1_ref, b1_ref, fr1_ref, w2_ref, b2_ref, fr2_ref, w3_ref, b3_ref, fr3_ref,
                   w4_ref, dl_ref, o_ref):
    hp = lax.Precision.HIGHEST
    z = z_ref[...]
    t = z[:, 0:1]
    h = jnp.sin(fr1_ref[...] * (jnp.dot(z, w1_ref[...], precision=hp, preferred_element_type=f32) + b1_ref[...]))
    h = jnp.sin(fr2_ref[...] * (jnp.dot(h, w2_ref[...], precision=hp, preferred_element_type=f32) + b2_ref[...]))
    h = jnp.sin(fr3_ref[...] * (jnp.dot(h, w3_ref[...], precision=hp, preferred_element_type=f32) + b3_ref[...]))
    h_hi, h_lo = _split(h)
    h = jnp.dot(jnp.concatenate([h_hi, h_lo, h_hi], axis=1), w4_ref[...], preferred_element_type=f32)
    h = h * jnp.exp(-t * dl_ref[...])
    col = lax.broadcasted_iota(jnp.int32, h.shape, 1)
    o_ref[...] = jnp.where((col >= D_HYENA) & (t <= 0.0), 0.0, h)


def _filter(z, deltas, w1, b1, fr1, w2, b2, fr2, w3, b3, fr3, w4):
    hp = LANES - FILT_HIDDEN
    w1p = jnp.pad(w1, ((0, LANES - FILT_EMB), (0, hp)))
    w2p = jnp.pad(w2, ((0, hp), (0, hp)))
    w3p = jnp.pad(w3, ((0, hp), (0, hp)))
    w4_hi, w4_lo = _split(jnp.pad(w4, ((0, hp), (0, 0))))
    w4p = jnp.concatenate([w4_hi, w4_hi, w4_lo], axis=0)
    vec = lambda v: jnp.pad(v, (0, hp))[None, :]
    args = (z, w1p, vec(b1), vec(fr1), w2p, vec(b2), vec(fr2), w3p, vec(b3), vec(fr3), w4p, deltas)
    specs = [pl.BlockSpec((None, TILE, LANES), lambda t: (t, 0, 0))] + [_rep(a.shape) for a in args[1:]]
    return pl.pallas_call(
        _filter_kernel,
        out_shape=jax.ShapeDtypeStruct((NTILE, TILE, 2 * D_HYENA), f32),
        grid=(NTILE,),
        in_specs=specs,
        out_specs=pl.BlockSpec((None, TILE, 2 * D_HYENA), lambda t: (t, 0, 0)),
        name="hyena_filter",
    )(*args)


def _stage1(src_ref, g3_ref, s_ref):
    pad = jnp.zeros((NI, LANES), bf16)

    def body(j, c):
        x = src_ref[:, pl.ds(pl.multiple_of(j * IL, IL), IL), :].reshape(NI, LANES)
        hi, lo = _split(x)
        rhs = jnp.concatenate([hi, lo, hi, pad], axis=0)
        a = jnp.dot(g3_ref[j], rhs, preferred_element_type=f32)
        s_ref[pl.ds(j * PITCH, SROWS), :] = a
        return c

    lax.fori_loop(0, N1, body, 0, unroll=EDGE_UNROLL)


def _dot3(f_ref, a):
    hi, lo = _split(a)
    return jnp.dot(f_ref[...], jnp.concatenate([hi, lo, hi], axis=0), preferred_element_type=f32)


def _load_pair(s_ref, kk):
    k = 2 * kk
    parts = [jnp.concatenate([s_ref[pl.ds(off + k, N1, stride=PITCH), :],
                              s_ref[pl.ds(off + k + 1, N1, stride=PITCH), :]], axis=1) for off in (0, K1P)]
    return jnp.concatenate(parts, axis=0)


def _pair_rows(kk):
    return pl.ds(pl.multiple_of(kk * 2 * N1, 2 * N1), 2 * N1)


def _spectrum_kernel(hf_ref, hb_ref, g3_ref, f2_ref, o_ref, s_ref):
    _stage1(hf_ref, g3_ref, s_ref)

    def fwd(kk, c):
        o_ref[_pair_rows(kk), :] = _dot3(f2_ref, _load_pair(s_ref, kk))
        return c

    lax.fori_loop(0, KPAIRS, fwd, 0, unroll=3)
    _stage1(hb_ref, g3_ref, s_ref)

    def bwd(kk, c):
        a = _dot3(f2_ref, _load_pair(s_ref, kk))
        r0 = pl.multiple_of(kk * 2 * N1, 2 * N1)
        o_ref[pl.ds(r0, N1), :] = o_ref[pl.ds(r0, N1), :] + a[:N1]
        o_ref[pl.ds(r0 + N1, N1), :] = o_ref[pl.ds(r0 + N1, N1), :] - a[N1:]
        return c

    lax.fori_loop(0, KPAIRS, bwd, 0, unroll=3)


def _spectrum(h, g3, f2):
    return pl.pallas_call(
        _spectrum_kernel,
        out_shape=jax.ShapeDtypeStruct((NSLAB, KPAIRS * 2 * N1, 2 * LANES), f32),
        grid=(NSLAB,),
        in_specs=[
            pl.BlockSpec((NTILE, TILE, LANES), lambda c: (0, 0, c)),
            pl.BlockSpec((NTILE, TILE, LANES), lambda c: (0, 0, NSLAB + c)),
            _rep(g3.shape),
            _rep(f2.shape),
        ],
        out_specs=pl.BlockSpec((None, KPAIRS * 2 * N1, 2 * LANES), lambda c: (c, 0, 0)),
        scratch_shapes=[pltpu.VMEM((N1 * PITCH, LANES), f32)],
        compiler_params=pltpu.CompilerParams(vmem_limit_bytes=VMEM_LIMIT),
        name="hyena_spectrum",
    )(h, h, g3, f2)


def _conv_kernel(w_ref, kf_ref, skip_ref, g3_ref, h1_ref, f2_ref, f2i_ref, o_ref, s_ref):
    _stage1(w_ref, g3_ref, s_ref)

    def mid(kk, c):
        a = jnp.dot(f2_ref[...], _load_pair(s_ref, kk).astype(bf16), preferred_element_type=f32)
        kf = kf_ref[_pair_rows(kk), :]
        xr, xi, kr, ki = a[:N1], a[N1:], kf[:N1], kf[N1:]
        y = jnp.concatenate([xr * kr - xi * ki, xr * ki + xi * kr], axis=0)
        b = jnp.dot(f2i_ref[...], y.astype(bf16), preferred_element_type=f32)
        k = 2 * kk
        for p in range(2):
            s_ref[pl.ds(k + p, N1, stride=PITCH), :] = b[:N1, LANES * p:LANES * (p + 1)]
            s_ref[pl.ds(K1P + k + p, N1, stride=PITCH), :] = b[N1:, LANES * p:LANES * (p + 1)]
        return c

    lax.fori_loop(0, KPAIRS, mid, 0, unroll=MID_UNROLL)
    pad = jnp.zeros((2 * N1 - SROWS, LANES), bf16)
    skip = skip_ref[...]

    def last(j, c):
        b = s_ref[pl.ds(j * PITCH, SROWS), :].astype(bf16)
        y = jnp.dot(h1_ref[j], jnp.concatenate([b, pad], axis=0), preferred_element_type=f32)
        rows = pl.ds(pl.multiple_of(j * IL, IL), IL)
        x = w_ref[:, rows, :].reshape(NI, LANES)
        o_ref[:, rows, :] = (y + x * skip).reshape(NTILE, IL, LANES)
        return c

    lax.fori_loop(0, N1, last, 0, unroll=EDGE_UNROLL)


def _long_conv(wp, kf, skip, g3, h1, f2, f2i):
    nb = wp.shape[0]
    blk = pl.BlockSpec((None, NTILE, None, TILE, LANES), lambda c, b: (b, 0, c, 0, 0))
    return pl.pallas_call(
        _conv_kernel,
        out_shape=jax.ShapeDtypeStruct(wp.shape, f32),
        grid=(NSLAB, nb),
        in_specs=[
            blk,
            pl.BlockSpec((None, KPAIRS * 2 * N1, 2 * LANES), lambda c, b: (c, 0, 0), pipeline_mode=pl.Buffered(1)),
            pl.BlockSpec((None, 1, LANES), lambda c, b: (c, 0, 0)),
            _rep(g3.shape),
            _rep(h1.shape),
            _rep(f2.shape),
            _rep(f2i.shape),
        ],
        out_specs=blk,
        scratch_shapes=[pltpu.VMEM((N1 * PITCH, LANES), f32)],
        compiler_params=pltpu.CompilerParams(vmem_limit_bytes=VMEM_LIMIT),
        name="hyena_long_conv",
    )(wp, kf, skip, g3, h1, f2, f2i)


def _rms(v, g):
    return v * lax.rsqrt(jnp.mean(v * v, axis=-1, keepdims=True) + EPS) * g


def _kv_kernel(mem_ref, g_ref, wkt_ref, wv_ref, kbd_ref, vbd_ref):
    mn = _rms(mem_ref[...], g_ref[...]).astype(bf16)
    kt = lax.dot_general(wkt_ref[...], mn, (((1,), (1,)), ((), ())), preferred_element_type=f32)
    kt = kt * (HEAD_DIM ** -0.5)
    v = jnp.dot(mn, wv_ref[...], preferred_element_type=f32)
    kb = jnp.concatenate([kt] * N_XHEADS, axis=1)
    hd_bits, mem_bits = HEAD_DIM.bit_length() - 1, N_MEM.bit_length() - 1
    r = lax.broadcasted_iota(jnp.int32, kb.shape, 0) >> hd_bits
    c = lax.broadcasted_iota(jnp.int32, kb.shape, 1) >> mem_bits
    kbd_ref[...] = jnp.where(r == c, kb, 0.0).astype(bf16)
    vb = jnp.concatenate([v] * N_XHEADS, axis=0)
    r = lax.broadcasted_iota(jnp.int32, vb.shape, 0) >> mem_bits
    c = lax.broadcasted_iota(jnp.int32, vb.shape, 1) >> hd_bits
    vbd_ref[...] = jnp.where(r == c, vb, 0.0).astype(bf16)


def _memory_kv(mem, g, wkt, wv):
    nb = mem.shape[0]
    return pl.pallas_call(
        _kv_kernel,
        out_shape=(jax.ShapeDtypeStruct((nb, D_XATTN, N_XHEADS * N_MEM), bf16),
                   jax.ShapeDtypeStruct((nb, N_XHEADS * N_MEM, D_XATTN), bf16)),
        grid=(nb,),
        in_specs=[pl.BlockSpec((None, N_MEM, D_MODEL), lambda b: (b, 0, 0)), _rep(g.shape), _rep(wkt.shape),
                  _rep(wv.shape)],
        out_specs=(pl.BlockSpec((None, D_XATTN, N_XHEADS * N_MEM), lambda b: (b, 0, 0)),
                   pl.BlockSpec((None, N_XHEADS * N_MEM, D_XATTN), lambda b: (b, 0, 0))),
        name="memory_kv",
    )(mem, g, wkt, wv)


def _silu(v):
    return v * jax.nn.sigmoid(v)


def _mix_kernel(x_ref, xp_ref, xn_ref, ng_ref, win_ref, sw_ref, sb_ref, dw_ref, db_ref, lng_ref, lnb_ref,
                kbd_ref, vbd_ref, w_ref, g_ref, y_ref, hext, pbuf, glu):
    t = pl.program_id(1)
    ng = ng_ref[...]
    hext[0:HALO, :] = _rms(xp_ref[...] * (t > 0).astype(f32), ng).astype(bf16)
    hext[HALO + TILE:EXT, :] = _rms(xn_ref[...] * (t < NTILE - 1).astype(f32), ng).astype(bf16)

    def project(b0, b1):
        m0, m1 = max(b0, HALO), min(b1, HALO + TILE)
        hext[m0:m1, :] = _rms(x_ref[m0 - HALO:m1 - HALO, :], ng).astype(bf16)
        order = sorted(range(0, D_IN, PROJ_COLS), key=lambda c: (not C_CA <= c < C_CZ, not C_XQ <= c < C_XZ, c))
        for c0 in order:
            width = min(PROJ_COLS, D_IN - c0)
            res = jnp.dot(hext[b0:b1, :], win_ref[:, c0:c0 + width], preferred_element_type=f32)
            for i in range(width // LANES):
                pbuf[c0 // LANES + i, b0:b1, :] = res[:, LANES * i:LANES * (i + 1)]
        for c in range(D_CONF // LANES):
            glu[c, b0:b1, :] = (pbuf[C_CA // LANES + c, b0:b1, :]
                                * jax.nn.sigmoid(pbuf[(C_CA + D_CONF) // LANES + c, b0:b1, :]))

    def cols(c0, width, rows):
        return jnp.concatenate([pbuf[c0 // LANES + i, rows, :] for i in range(width // LANES)], axis=1)

    kbd = kbd_ref[...]
    vbd = vbd_ref[...]
    bounds = [0] + [2 * HALO + 2 * N1 * b for b in range(1, IL // 2)] + [EXT]
    done = 0
    for r in range(IL):
        while bounds[done] < N1 * r + N1 + 2 * HALO:
            project(bounds[done], bounds[done + 1])
            done += 1
        e0 = HALO + N1 * r
        rows = slice(e0, e0 + N1)
        orow = slice(N1 * r, N1 * (r + 1))

        def sconv(slab):
            lanes = slice(LANES * slab, LANES * (slab + 1))
            acc = sb_ref[:, lanes]
            for k in range(SHORT_K):
                acc = acc + pbuf[slab, e0 - 1 + k:e0 - 1 + k + N1, :] * sw_ref[k:k + 1, lanes]
            return acc

        for s in range(NSLAB):
            wv = sconv(2 * NSLAB + s) * sconv(NSLAB + s)
            w_ref[s, pl.ds(r, N1, stride=IL), :] = wv
            gate = sconv(s) * _silu(pbuf[C_HZ // LANES + s, rows, :])
            g_ref[orow, LANES * s:LANES * (s + 1)] = gate.astype(bf16)

        accs = []
        for c in range(D_CONF // LANES):
            lanes = slice(LANES * c, LANES * (c + 1))
            acc = db_ref[:, lanes]
            for k in range(CONF_K):
                lo = e0 - CONF_K // 2 + k
                acc = acc + glu[c, lo:lo + N1, :] * dw_ref[k:k + 1, lanes]
            accs.append(acc)
        acc = jnp.concatenate(accs, axis=1)
        mu = jnp.mean(acc, axis=-1, keepdims=True)
        xc = acc - mu
        ln = xc * lax.rsqrt(jnp.mean(xc * xc, axis=-1, keepdims=True) + EPS) * lng_ref[...] + lnb_ref[...]
        y_ref[orow, 0:D_CONF] = (_silu(ln) * _silu(cols(C_CZ, D_CONF, rows))).astype(bf16)

        q = cols(C_XQ, D_XATTN, rows).astype(bf16)
        s_all = jnp.dot(q, kbd, preferred_element_type=f32)
        ps = []
        for h in range(N_XHEADS):
            sh = s_all[:, N_MEM * h:N_MEM * (h + 1)]
            e = jnp.exp(sh - jnp.max(sh, axis=-1, keepdims=True))
            ps.append((e / jnp.sum(e, axis=-1, keepdims=True)).astype(bf16))
        o = jnp.dot(jnp.concatenate(ps, axis=1), vbd, preferred_element_type=f32)
        y_ref[orow, D_CONF:D_CONF + D_XATTN] = (o * _silu(cols(C_XZ, D_XATTN, rows))).astype(bf16)


def _mix(x, ng, win, sw, sb, dw, db, lng, lnb, kbd, vbd):
    nb = x.shape[0]
    hb = TILE // HALO
    nhb = SEQ // HALO
    small = [ng, win, sw, sb, dw, db, lng, lnb]
    return pl.pallas_call(
        _mix_kernel,
        out_shape=(jax.ShapeDtypeStruct((nb, NTILE, NSLAB, TILE, LANES), f32),
                   jax.ShapeDtypeStruct((nb, SEQ, D_HYENA), bf16),
                   jax.ShapeDtypeStruct((nb, SEQ, D_CONF + D_XATTN), bf16)),
        grid=(nb, NTILE),
        in_specs=[
            pl.BlockSpec((None, TILE, D_MODEL), lambda b, t: (b, t, 0)),
            pl.BlockSpec((None, HALO, D_MODEL), lambda b, t: (b, jnp.maximum(t * hb - 1, 0), 0)),
            pl.BlockSpec((None, HALO, D_MODEL), lambda b, t: (b, jnp.minimum((t + 1) * hb, nhb - 1), 0)),
        ] + [_rep(a.shape) for a in small] + [
            pl.BlockSpec((None, D_XATTN, N_XHEADS * N_MEM), lambda b, t: (b, 0, 0)),
            pl.BlockSpec((None, N_XHEADS * N_MEM, D_XATTN), lambda b, t: (b, 0, 0)),
        ],
        out_specs=(pl.BlockSpec((None, None, NSLAB, TILE, LANES), lambda b, t: (b, t, 0, 0, 0)),
                   pl.BlockSpec((None, TILE, D_HYENA), lambda b, t: (b, t, 0)),
                   pl.BlockSpec((None, TILE, D_CONF + D_XATTN), lambda b, t: (b, t, 0))),
        scratch_shapes=[pltpu.VMEM((EXT, D_MODEL), bf16), pltpu.VMEM((D_IN // LANES, EXT, LANES), f32),
                        pltpu.VMEM((D_CONF // LANES, EXT, LANES), f32)],
        compiler_params=pltpu.CompilerParams(vmem_limit_bytes=VMEM_LIMIT),
        name="token_mix",
    )(x, x, x, *small, kbd, vbd)


def _out_kernel(x_ref, g_ref, c_ref, y_ref, wout_ref, fg_ref, o_ref, mix, *, final):
    for r in range(IL):
        rows = slice(N1 * r, N1 * (r + 1))
        for s in range(NSLAB):
            cols = slice(LANES * s, LANES * (s + 1))
            conv = c_ref[s, pl.ds(r, N1, stride=IL), :]
            mix[rows, cols] = (g_ref[rows, cols].astype(f32) * conv).astype(bf16)
    mix[:, D_HYENA:] = y_ref[...]
    xn = x_ref[...] + jnp.dot(mix[...], wout_ref[...], preferred_element_type=f32)
    o_ref[...] = _rms(xn, fg_ref[...]) if final else xn


def _out_proj(x, g, cp, y, wout, fg, final):
    nb = x.shape[0]
    return pl.pallas_call(
        functools.partial(_out_kernel, final=final),
        out_shape=jax.ShapeDtypeStruct(x.shape, f32),
        grid=(nb, NTILE),
        in_specs=[
            pl.BlockSpec((None, TILE, D_MODEL), lambda b, t: (b, t, 0)),
            pl.BlockSpec((None, TILE, D_HYENA), lambda b, t: (b, t, 0)),
            pl.BlockSpec((None, None, NSLAB, TILE, LANES), lambda b, t: (b, t, 0, 0, 0)),
            pl.BlockSpec((None, TILE, D_CONF + D_XATTN), lambda b, t: (b, t, 0)),
            _rep(wout.shape),
            _rep(fg.shape),
        ],
        out_specs=pl.BlockSpec((None, TILE, D_MODEL), lambda b, t: (b, t, 0)),
        scratch_shapes=[pltpu.VMEM((TILE, D_MODEL), bf16)],
        compiler_params=pltpu.CompilerParams(vmem_limit_bytes=VMEM_LIMIT),
        name="out_proj",
    )(x, g, cp, y, wout, fg)


def kernel(x_prompt, x_sample, mem_prompt, mem_sample, norm_g, mem_norm_g, w_in, hy_short_w, hy_short_b, hy_f_w1, hy_f_b1, hy_f_fr1, hy_f_w2, hy_f_b2, hy_f_fr2, hy_f_w3, hy_f_b3, hy_f_fr3, hy_f_w4, hy_skip, cf_dw_w, cf_dw_b, cf_ln_g, cf_ln_b, xa_w_kv, w_out, final_g):
    g3, h1, f2x3, f2, f2i = _dft_tables()
    z, deltas = _positional()
    row = lambda v: v[None, :]
    xs = [x_prompt, x_sample]
    mems = [mem_prompt, mem_sample]
    for l in range(DEPTH):
        h = _filter(z, deltas, hy_f_w1[l], hy_f_b1[l], hy_f_fr1[l], hy_f_w2[l], hy_f_b2[l], hy_f_fr2[l],
                    hy_f_w3[l], hy_f_b3[l], hy_f_fr3[l], hy_f_w4[l])
        kf = _spectrum(h, g3, f2x3)
        win, wout = w_in[l].astype(bf16), w_out[l].astype(bf16)
        wkt, wv = xa_w_kv[l][:, :D_XATTN].T.astype(bf16), xa_w_kv[l][:, D_XATTN:].astype(bf16)
        skip = hy_skip[l].reshape(NSLAB, 1, LANES)
        for i in range(len(xs)):
            kbd, vbd = _memory_kv(mems[i], row(mem_norm_g[l]), wkt, wv)
            wp, g, y = _mix(xs[i], row(norm_g[l]), win, hy_short_w[l], row(hy_short_b[l]), cf_dw_w[l],
                            row(cf_dw_b[l]), row(cf_ln_g[l]), row(cf_ln_b[l]), kbd, vbd)
            cp = _long_conv(wp, kf, skip, g3, h1, f2, f2i)
            xs[i] = _out_proj(xs[i], g, cp, y, wout, row(final_g), final=(l == DEPTH - 1))
    return xs[0], xs[1]
```

```python
import functools
import math

import jax
import jax.numpy as jnp
from jax import lax
from jax.experimental import pallas as pl
from jax.experimental.pallas import tpu as pltpu

f32 = jnp.float32
bf16 = jnp.bfloat16

D_MODEL = 1024
SEQ = 8192
DEPTH = 4
N_MEM = 256
HEAD_DIM = 64
D_HYENA = 512
D_CONF = 256
D_XATTN = 256
N_XHEADS = D_XATTN // HEAD_DIM
SHORT_K = 3
CONF_K = 31
FILT_BANDS = 16
FILT_EMB = 1 + 2 * FILT_BANDS
FILT_HIDDEN = 64
DECAY_TARGET = 1e-2
FAST_DECAY_PCT = 0.3
SLOW_DECAY_PCT = 1.5
EPS = 1e-6
P_HY_U = 3 * D_HYENA
D_IN = P_HY_U + D_HYENA + 2 * D_CONF + D_CONF + D_XATTN + D_XATTN
C_HZ = P_HY_U
C_CA = C_HZ + D_HYENA
C_CZ = C_CA + 2 * D_CONF
C_XQ = C_CZ + D_CONF
C_XZ = C_XQ + D_XATTN

LANES = 128
N1 = 128
NFFT = N1 * N1
NI = SEQ // N1
K1 = N1 // 2 + 1
K1P = 72
KPAIRS = (K1 + 1) // 2
SROWS = 2 * K1P
PITCH = 148
PROJ_COLS = 256
EDGE_UNROLL = 32
MID_UNROLL = 11
TILE = 1024
NTILE = SEQ // TILE
IL = TILE // N1
NSLAB = D_HYENA // LANES
HALO = 16
EXT = TILE + 2 * HALO
VMEM_LIMIT = 60 * 1024 * 1024


def _split(x):
    hi = x.astype(bf16)
    lo = (x - hi.astype(f32)).astype(bf16)
    return hi, lo


def _rep(shape):
    return pl.BlockSpec(shape, lambda *_: (0,) * len(shape), pipeline_mode=pl.Buffered(1))


def _dft_tables():
    k1 = jnp.arange(K1P, dtype=jnp.int32)
    i = jnp.arange(NI, dtype=jnp.int32)
    j = jnp.arange(N1, dtype=jnp.int32)
    valid = (k1 < K1).astype(f32)
    scale = 2.0 * math.pi / NFFT
    ph = (j[:, None, None] * k1[None, :, None] + N1 * k1[None, :, None] * i[None, None, :]) % NFFT
    ang = ph.astype(f32) * scale
    g = jnp.concatenate([jnp.cos(ang) * valid[None, :, None], -jnp.sin(ang) * valid[None, :, None]], axis=1)
    g2 = jnp.concatenate([g[:N1 // 2], g[N1 // 2:]], axis=2).astype(bf16)
    cw = jnp.where((k1 == 0) | (k1 == N1 // 2), 1.0, 2.0) * valid / NFFT
    ph4 = (N1 * i[None, :, None] * k1[None, None, :] + j[:, None, None] * k1[None, None, :]) % NFFT
    ang4 = ph4.astype(f32) * scale
    h = jnp.concatenate([jnp.cos(ang4) * cw, -jnp.sin(ang4) * cw], axis=2)
    h1 = jnp.concatenate([h.astype(bf16), jnp.zeros((N1, NI, 2 * N1 - SROWS), bf16)], axis=2)
    th = ((j[:, None] * j[None, :]) % N1).astype(f32) * (2.0 * math.pi / N1)
    c, s = jnp.cos(th), jnp.sin(th)
    fwd = jnp.concatenate([jnp.concatenate([c, s], 1), jnp.concatenate([-s, c], 1)], 0)
    inv = jnp.concatenate([jnp.concatenate([c, -s], 1), jnp.concatenate([s, c], 1)], 0)

    return g2, h1, fwd.astype(bf16), inv.astype(bf16)


def _positional():
    t = jnp.linspace(0.0, 1.0, SEQ, dtype=f32)[:, None]
    n = jnp.arange(SEQ, dtype=f32)[:, None]
    bands = jnp.linspace(1e-4, FILT_BANDS - 1, FILT_BANDS, dtype=f32)[None, :]
    ang = (2.0 * math.pi / SEQ) * bands * n
    z = jnp.concatenate([t, jnp.cos(ang), -jnp.sin(ang)], axis=-1)
    z = jnp.pad(z, ((0, 0), (0, LANES - FILT_EMB)))
    z = z.reshape(NTILE, IL, N1, LANES).transpose(0, 2, 1, 3).reshape(NTILE, TILE, LANES)
    max_decay = math.log(DECAY_TARGET) / FAST_DECAY_PCT
    min_decay = math.log(DECAY_TARGET) / SLOW_DECAY_PCT
    deltas = jnp.abs(jnp.linspace(min_decay, max_decay, D_HYENA, dtype=f32))
    return z, jnp.concatenate([deltas, deltas])[None, :]


def _filter_kernel(z_ref, w1_ref, b1_ref, fr1_ref, w2_ref, b2_ref, fr2_ref, w3_ref, b3_ref, fr3_ref,
                   w4_ref, dl_ref, o_ref):
    hp = lax.Precision.HIGHEST
    z = z_ref[...]
    t = z[:, 0:1]
    h = jnp.sin(fr1_ref[...] * (jnp.dot(z, w1_ref[...], precision=hp, preferred_element_type=f32) + b1_ref[...]))
    h = jnp.sin(fr2_ref[...] * (jnp.dot(h, w2_ref[...], precision=hp, preferred_element_type=f32) + b2_ref[...]))
    h = jnp.sin(fr3_ref[...] * (jnp.dot(h, w3_ref[...], precision=hp, preferred_element_type=f32) + b3_ref[...]))
    h_hi, h_lo = _split(h)
    h = jnp.dot(jnp.concatenate([h_hi, h_lo, h_hi], axis=1), w4_ref[...], preferred_element_type=f32)
    h = h * jnp.exp(-t * dl_ref[...])
    col = lax.broadcasted_iota(jnp.int32, h.shape, 1)
    o_ref[...] = jnp.where((col >= D_HYENA) & (t <= 0.0), 0.0, h)


def _filter(z, deltas, w1, b1, fr1, w2, b2, fr2, w3, b3, fr3, w4):
    hp = LANES - FILT_HIDDEN
    w1p = jnp.pad(w1, ((0, LANES - FILT_EMB), (0, hp)))
    w2p = jnp.pad(w2, ((0, hp), (0, hp)))
    w3p = jnp.pad(w3, ((0, hp), (0, hp)))
    w4_hi, w4_lo = _split(jnp.pad(w4, ((0, hp), (0, 0))))
    w4p = jnp.concatenate([w4_hi, w4_hi, w4_lo], axis=0)
    vec = lambda v: jnp.pad(v, (0, hp))[None, :]
    args = (z, w1p, vec(b1), vec(fr1), w2p, vec(b2), vec(fr2), w3p, vec(b3), vec(fr3), w4p, deltas)
    specs = [pl.BlockSpec((None, TILE, LANES), lambda t: (t, 0, 0))] + [_rep(a.shape) for a in args[1:]]
    return pl.pallas_call(
        _filter_kernel,
        out_shape=jax.ShapeDtypeStruct((NTILE, TILE, 2 * D_HYENA), f32),
        grid=(NTILE,),
        in_specs=specs,
        out_specs=pl.BlockSpec((None, TILE, 2 * D_HYENA), lambda t: (t, 0, 0)),
        name="hyena_filter",
    )(*args)


def _src_rows(j):
    return pl.ds(pl.multiple_of(j * IL, IL), IL)


def _stage1(src_ref, g2_ref, s_ref):
    zero = jnp.zeros((NI, LANES), bf16)

    def body(p, c):
        q = p + N1 // 2
        xa = src_ref[:, _src_rows(p), :].reshape(NI, LANES).astype(bf16)
        xb = src_ref[:, _src_rows(q), :].reshape(NI, LANES).astype(bf16)
        rhs = jnp.concatenate([jnp.concatenate([xa, zero], axis=1), jnp.concatenate([zero, xb], axis=1)], axis=0)
        a = jnp.dot(g2_ref[p], rhs, preferred_element_type=f32)
        s_ref[pl.ds(p * PITCH, SROWS), :] = a[:, :LANES]
        s_ref[pl.ds(q * PITCH, SROWS), :] = a[:, LANES:]
        return c

    lax.fori_loop(0, N1 // 2, body, 0, unroll=EDGE_UNROLL // 2)


def _load_pair(s_ref, kk):
    k = 2 * kk
    parts = [jnp.concatenate([s_ref[pl.ds(off + k, N1, stride=PITCH), :],
                              s_ref[pl.ds(off + k + 1, N1, stride=PITCH), :]], axis=1) for off in (0, K1P)]
    return jnp.concatenate(parts, axis=0)


def _pair_rows(kk):
    return pl.ds(pl.multiple_of(kk * 2 * N1, 2 * N1), 2 * N1)


def _fast_dft(f2_ref, s_ref, kk):
    return jnp.dot(f2_ref[...], _load_pair(s_ref, kk).astype(bf16), preferred_element_type=f32)


def _spectrum_kernel(hf_ref, hb_ref, g2_ref, f2_ref, o_ref, s_ref):
    _stage1(hf_ref, g2_ref, s_ref)

    def fwd(kk, c):
        o_ref[_pair_rows(kk), :] = _fast_dft(f2_ref, s_ref, kk)
        return c

    lax.fori_loop(0, KPAIRS, fwd, 0, unroll=3)
    _stage1(hb_ref, g2_ref, s_ref)

    def bwd(kk, c):
        a = _fast_dft(f2_ref, s_ref, kk)
        r0 = pl.multiple_of(kk * 2 * N1, 2 * N1)
        o_ref[pl.ds(r0, N1), :] = o_ref[pl.ds(r0, N1), :] + a[:N1]
        o_ref[pl.ds(r0 + N1, N1), :] = o_ref[pl.ds(r0 + N1, N1), :] - a[N1:]
        return c

    lax.fori_loop(0, KPAIRS, bwd, 0, unroll=3)


def _spectrum(h, g2, f2):
    return pl.pallas_call(
        _spectrum_kernel,
        out_shape=jax.ShapeDtypeStruct((NSLAB, KPAIRS * 2 * N1, 2 * LANES), f32),
        grid=(NSLAB,),
        in_specs=[
            pl.BlockSpec((NTILE, TILE, LANES), lambda c: (0, 0, c)),
            pl.BlockSpec((NTILE, TILE, LANES), lambda c: (0, 0, NSLAB + c)),
            _rep(g2.shape),
            _rep(f2.shape),
        ],
        out_specs=pl.BlockSpec((None, KPAIRS * 2 * N1, 2 * LANES), lambda c: (c, 0, 0)),
        scratch_shapes=[pltpu.VMEM((N1 * PITCH, LANES), f32)],
        compiler_params=pltpu.CompilerParams(vmem_limit_bytes=VMEM_LIMIT),
        name="hyena_spectrum",
    )(h, h, g2, f2)


def _conv_kernel(w_ref, kf_ref, skip_ref, g2_ref, h1_ref, f2_ref, f2i_ref, o_ref, s_ref):
    _stage1(w_ref, g2_ref, s_ref)

    def mid(kk, c):
        a = _fast_dft(f2_ref, s_ref, kk)
        kf = kf_ref[_pair_rows(kk), :]
        xr, xi, kr, ki = a[:N1], a[N1:], kf[:N1], kf[N1:]
        y = jnp.concatenate([xr * kr - xi * ki, xr * ki + xi * kr], axis=0)
        b = jnp.dot(f2i_ref[...], y.astype(bf16), preferred_element_type=f32)
        k = 2 * kk
        for p in range(2):
            s_ref[pl.ds(k + p, N1, stride=PITCH), :] = b[:N1, LANES * p:LANES * (p + 1)]
            s_ref[pl.ds(K1P + k + p, N1, stride=PITCH), :] = b[N1:, LANES * p:LANES * (p + 1)]
        return c

    lax.fori_loop(0, KPAIRS, mid, 0, unroll=MID_UNROLL)
    pad = jnp.zeros((2 * N1 - SROWS, LANES), bf16)
    skip = skip_ref[...]

    def last(j, c):
        b = s_ref[pl.ds(j * PITCH, SROWS), :].astype(bf16)
        y = jnp.dot(h1_ref[j], jnp.concatenate([b, pad], axis=0), preferred_element_type=f32)
        rows = _src_rows(j)
        x = w_ref[:, rows, :].reshape(NI, LANES)
        o_ref[:, rows, :] = (y + x * skip).reshape(NTILE, IL, LANES)
        return c

    lax.fori_loop(0, N1, last, 0, unroll=EDGE_UNROLL)


def _long_conv(wp, kf, skip, g2, h1, f2, f2i):
    nb = wp.shape[0]
    blk = pl.BlockSpec((None, NTILE, None, TILE, LANES), lambda c, b: (b, 0, c, 0, 0))
    return pl.pallas_call(
        _conv_kernel,
        out_shape=jax.ShapeDtypeStruct(wp.shape, f32),
        grid=(NSLAB, nb),
        in_specs=[
            blk,
            pl.BlockSpec((None, KPAIRS * 2 * N1, 2 * LANES), lambda c, b: (c, 0, 0), pipeline_mode=pl.Buffered(1)),
            pl.BlockSpec((None, 1, LANES), lambda c, b: (c, 0, 0)),
            _rep(g2.shape),
            _rep(h1.shape),
            _rep(f2.shape),
            _rep(f2i.shape),
        ],
        out_specs=blk,
        scratch_shapes=[pltpu.VMEM((N1 * PITCH, LANES), f32)],
        compiler_params=pltpu.CompilerParams(vmem_limit_bytes=VMEM_LIMIT),
        name="hyena_long_conv",
    )(wp, kf, skip, g2, h1, f2, f2i)


def _rms(v, g):
    return v * lax.rsqrt(jnp.mean(v * v, axis=-1, keepdims=True) + EPS) * g


def _kv_kernel(mem_ref, g_ref, wkt_ref, wv_ref, kbd_ref, vbd_ref):
    mn = _rms(mem_ref[...], g_ref[...]).astype(bf16)
    kt = lax.dot_general(wkt_ref[...], mn, (((1,), (1,)), ((), ())), preferred_element_type=f32)
    kt = kt * (HEAD_DIM ** -0.5)
    v = jnp.dot(mn, wv_ref[...], preferred_element_type=f32)
    kb = jnp.concatenate([kt] * N_XHEADS, axis=1)
    hd_bits, mem_bits = HEAD_DIM.bit_length() - 1, N_MEM.bit_length() - 1
    r = lax.broadcasted_iota(jnp.int32, kb.shape, 0) >> hd_bits
    c = lax.broadcasted_iota(jnp.int32, kb.shape, 1) >> mem_bits
    kbd_ref[...] = jnp.where(r == c, kb, 0.0).astype(bf16)
    vb = jnp.concatenate([v] * N_XHEADS, axis=0)
    r = lax.broadcasted_iota(jnp.int32, vb.shape, 0) >> mem_bits
    c = lax.broadcasted_iota(jnp.int32, vb.shape, 1) >> hd_bits
    vbd_ref[...] = jnp.where(r == c, vb, 0.0).astype(bf16)


def _memory_kv(mem, g, wkt, wv):
    nb = mem.shape[0]
    return pl.pallas_call(
        _kv_kernel,
        out_shape=(jax.ShapeDtypeStruct((nb, D_XATTN, N_XHEADS * N_MEM), bf16),
                   jax.ShapeDtypeStruct((nb, N_XHEADS * N_MEM, D_XATTN), bf16)),
        grid=(nb,),
        in_specs=[pl.BlockSpec((None, N_MEM, D_MODEL), lambda b: (b, 0, 0)), _rep(g.shape), _rep(wkt.shape),
                  _rep(wv.shape)],
        out_specs=(pl.BlockSpec((None, D_XATTN, N_XHEADS * N_MEM), lambda b: (b, 0, 0)),
                   pl.BlockSpec((None, N_XHEADS * N_MEM, D_XATTN), lambda b: (b, 0, 0))),
        name="memory_kv",
    )(mem, g, wkt, wv)


def _silu(v):
    return v * jax.nn.sigmoid(v)


def _mix_kernel(x_ref, xp_ref, xn_ref, ng_ref, win_ref, sw_ref, sb_ref, dw_ref, db_ref, lng_ref, lnb_ref,
                kbd_ref, vbd_ref, w_ref, g_ref, y_ref, hext, pbuf, glu):
    t = pl.program_id(1)
    ng = ng_ref[...]
    hext[0:HALO, :] = _rms(xp_ref[...] * (t > 0).astype(f32), ng).astype(bf16)
    hext[HALO + TILE:EXT, :] = _rms(xn_ref[...] * (t < NTILE - 1).astype(f32), ng).astype(bf16)

    def project(b0, b1):
        m0, m1 = max(b0, HALO), min(b1, HALO + TILE)
        hext[m0:m1, :] = _rms(x_ref[m0 - HALO:m1 - HALO, :], ng).astype(bf16)
        order = sorted(range(0, D_IN, PROJ_COLS), key=lambda c: (not C_CA <= c < C_CZ, not C_XQ <= c < C_XZ, c))
        for c0 in order:
            width = min(PROJ_COLS, D_IN - c0)
            res = jnp.dot(hext[b0:b1, :], win_ref[:, c0:c0 + width], preferred_element_type=f32)
            for i in range(width // LANES):
                pbuf[c0 // LANES + i, b0:b1, :] = res[:, LANES * i:LANES * (i + 1)]
        for c in range(D_CONF // LANES):
            glu[c, b0:b1, :] = (pbuf[C_CA // LANES + c, b0:b1, :]
                                * jax.nn.sigmoid(pbuf[(C_CA + D_CONF) // LANES + c, b0:b1, :]))

    def cols(c0, width, rows):
        return jnp.concatenate([pbuf[c0 // LANES + i, rows, :] for i in range(width // LANES)], axis=1)

    kbd = kbd_ref[...]
    vbd = vbd_ref[...]
    bounds = [0] + [2 * HALO + 2 * N1 * b for b in range(1, IL // 2)] + [EXT]
    done = 0
    for r in range(IL):
        while bounds[done] < N1 * r + N1 + 2 * HALO:
            project(bounds[done], bounds[done + 1])
            done += 1
        e0 = HALO + N1 * r
        rows = slice(e0, e0 + N1)
        orow = slice(N1 * r, N1 * (r + 1))

        def sconv(slab):
            lanes = slice(LANES * slab, LANES * (slab + 1))
            acc = sb_ref[:, lanes]
            for k in range(SHORT_K):
                acc = acc + pbuf[slab, e0 - 1 + k:e0 - 1 + k + N1, :] * sw_ref[k:k + 1, lanes]
            return acc

        for s in range(NSLAB):
            wv = sconv(2 * NSLAB + s) * sconv(NSLAB + s)
            w_ref[s, pl.ds(r, N1, stride=IL), :] = wv
            gate = sconv(s) * _silu(pbuf[C_HZ // LANES + s, rows, :])
            g_ref[orow, LANES * s:LANES * (s + 1)] = gate.astype(bf16)

        accs = []
        for c in range(D_CONF // LANES):
            lanes = slice(LANES * c, LANES * (c + 1))
            acc = db_ref[:, lanes]
            for k in range(CONF_K):
                lo = e0 - CONF_K // 2 + k
                acc = acc + glu[c, lo:lo + N1, :] * dw_ref[k:k + 1, lanes]
            accs.append(acc)
        acc = jnp.concatenate(accs, axis=1)
        mu = jnp.mean(acc, axis=-1, keepdims=True)
        xc = acc - mu
        ln = xc * lax.rsqrt(jnp.mean(xc * xc, axis=-1, keepdims=True) + EPS) * lng_ref[...] + lnb_ref[...]
        y_ref[orow, 0:D_CONF] = (_silu(ln) * _silu(cols(C_CZ, D_CONF, rows))).astype(bf16)

        q = cols(C_XQ, D_XATTN, rows).astype(bf16)
        s_all = jnp.dot(q, kbd, preferred_element_type=f32)
        ps = []
        for h in range(N_XHEADS):
            sh = s_all[:, N_MEM * h:N_MEM * (h + 1)]
            e = jnp.exp(sh - jnp.max(sh, axis=-1, keepdims=True))
            ps.append((e / jnp.sum(e, axis=-1, keepdims=True)).astype(bf16))
        o = jnp.dot(jnp.concatenate(ps, axis=1), vbd, preferred_element_type=f32)
        y_ref[orow, D_CONF:D_CONF + D_XATTN] = (o * _silu(cols(C_XZ, D_XATTN, rows))).astype(bf16)


def _mix(x, ng, win, sw, sb, dw, db, lng, lnb, kbd, vbd):
    nb = x.shape[0]
    hb = TILE // HALO
    nhb = SEQ // HALO
    small = [ng, win, sw, sb, dw, db, lng, lnb]
    return pl.pallas_call(
        _mix_kernel,
        out_shape=(jax.ShapeDtypeStruct((nb, NTILE, NSLAB, TILE, LANES), f32),
                   jax.ShapeDtypeStruct((nb, SEQ, D_HYENA), bf16),
                   jax.ShapeDtypeStruct((nb, SEQ, D_CONF + D_XATTN), bf16)),
        grid=(nb, NTILE),
        in_specs=[
            pl.BlockSpec((None, TILE, D_MODEL), lambda b, t: (b, t, 0)),
            pl.BlockSpec((None, HALO, D_MODEL), lambda b, t: (b, jnp.maximum(t * hb - 1, 0), 0)),
            pl.BlockSpec((None, HALO, D_MODEL), lambda b, t: (b, jnp.minimum((t + 1) * hb, nhb - 1), 0)),
        ] + [_rep(a.shape) for a in small] + [
            pl.BlockSpec((None, D_XATTN, N_XHEADS * N_MEM), lambda b, t: (b, 0, 0)),
            pl.BlockSpec((None, N_XHEADS * N_MEM, D_XATTN), lambda b, t: (b, 0, 0)),
        ],
        out_specs=(pl.BlockSpec((None, None, NSLAB, TILE, LANES), lambda b, t: (b, t, 0, 0, 0)),
                   pl.BlockSpec((None, TILE, D_HYENA), lambda b, t: (b, t, 0)),
                   pl.BlockSpec((None, TILE, D_CONF + D_XATTN), lambda b, t: (b, t, 0))),
        scratch_shapes=[pltpu.VMEM((EXT, D_MODEL), bf16), pltpu.VMEM((D_IN // LANES, EXT, LANES), f32),
                        pltpu.VMEM((D_CONF // LANES, EXT, LANES), f32)],
        compiler_params=pltpu.CompilerParams(vmem_limit_bytes=VMEM_LIMIT),
        name="token_mix",
    )(x, x, x, *small, kbd, vbd)


def _out_kernel(x_ref, g_ref, c_ref, y_ref, wout_ref, fg_ref, o_ref, mix, *, final):
    for r in range(IL):
        rows = slice(N1 * r, N1 * (r + 1))
        for s in range(NSLAB):
            cols = slice(LANES * s, LANES * (s + 1))
            conv = c_ref[s, pl.ds(r, N1, stride=IL), :]
            mix[rows, cols] = (g_ref[rows, cols].astype(f32) * conv).astype(bf16)
    mix[:, D_HYENA:] = y_ref[...]
    xn = x_ref[...] + jnp.dot(mix[...], wout_ref[...], preferred_element_type=f32)
    o_ref[...] = _rms(xn, fg_ref[...]) if final else xn


def _out_proj(x, g, cp, y, wout, fg, final):
    nb = x.shape[0]
    return pl.pallas_call(
        functools.partial(_out_kernel, final=final),
        out_shape=jax.ShapeDtypeStruct(x.shape, f32),
        grid=(nb, NTILE),
        in_specs=[
            pl.BlockSpec((None, TILE, D_MODEL), lambda b, t: (b, t, 0)),
            pl.BlockSpec((None, TILE, D_HYENA), lambda b, t: (b, t, 0)),
            pl.BlockSpec((None, None, NSLAB, TILE, LANES), lambda b, t: (b, t, 0, 0, 0)),
            pl.BlockSpec((None, TILE, D_CONF + D_XATTN), lambda b, t: (b, t, 0)),
            _rep(wout.shape),
            _rep(fg.shape),
        ],
        out_specs=pl.BlockSpec((None, TILE, D_MODEL), lambda b, t: (b, t, 0)),
        scratch_shapes=[pltpu.VMEM((TILE, D_MODEL), bf16)],
        compiler_params=pltpu.CompilerParams(vmem_limit_bytes=VMEM_LIMIT),
        name="out_proj",
    )(x, g, cp, y, wout, fg)


def kernel(x_prompt, x_sample, mem_prompt, mem_sample, norm_g, mem_norm_g, w_in, hy_short_w, hy_short_b, hy_f_w1, hy_f_b1, hy_f_fr1, hy_f_w2, hy_f_b2, hy_f_fr2, hy_f_w3, hy_f_b3, hy_f_fr3, hy_f_w4, hy_skip, cf_dw_w, cf_dw_b, cf_ln_g, cf_ln_b, xa_w_kv, w_out, final_g):
    g2, h1, f2, f2i = _dft_tables()
    z, deltas = _positional()
    row = lambda v: v[None, :]
    xs = [x_prompt, x_sample]
    mems = [mem_prompt, mem_sample]
    for l in range(DEPTH):
        h = _filter(z, deltas, hy_f_w1[l], hy_f_b1[l], hy_f_fr1[l], hy_f_w2[l], hy_f_b2[l], hy_f_fr2[l],
                    hy_f_w3[l], hy_f_b3[l], hy_f_fr3[l], hy_f_w4[l])
        kf = _spectrum(h, g2, f2)
        win, wout = w_in[l].astype(bf16), w_out[l].astype(bf16)
        wkt, wv = xa_w_kv[l][:, :D_XATTN].T.astype(bf16), xa_w_kv[l][:, D_XATTN:].astype(bf16)
        skip = hy_skip[l].reshape(NSLAB, 1, LANES)
        for i in range(len(xs)):
            kbd, vbd = _memory_kv(mems[i], row(mem_norm_g[l]), wkt, wv)
            wp, g, y = _mix(xs[i], row(norm_g[l]), win, hy_short_w[l], row(hy_short_b[l]), cf_dw_w[l],
                            row(cf_dw_b[l]), row(cf_ln_g[l]), row(cf_ln_b[l]), kbd, vbd)
            cp = _long_conv(wp, kf, skip, g2, h1, f2, f2i)
            xs[i] = _out_proj(xs[i], g, cp, y, wout, row(final_g), final=(l == DEPTH - 1))
    return xs[0], xs[1]
```

```python
import functools
import math

import jax
import jax.numpy as jnp
from jax import lax
from jax.experimental import pallas as pl
from jax.experimental.pallas import tpu as pltpu

f32 = jnp.float32
bf16 = jnp.bfloat16

D_MODEL = 1024
SEQ = 8192
DEPTH = 4
N_MEM = 256
HEAD_DIM = 64
D_HYENA = 512
D_CONF = 256
D_XATTN = 256
N_XHEADS = D_XATTN // HEAD_DIM
SHORT_K = 3
CONF_K = 31
FILT_BANDS = 16
FILT_EMB = 1 + 2 * FILT_BANDS
FILT_HIDDEN = 64
DECAY_TARGET = 1e-2
FAST_DECAY_PCT = 0.3
SLOW_DECAY_PCT = 1.5
EPS = 1e-6
P_HY_U = 3 * D_HYENA
D_IN = P_HY_U + D_HYENA + 2 * D_CONF + D_CONF + D_XATTN + D_XATTN
C_HZ = P_HY_U
C_CA = C_HZ + D_HYENA
C_CZ = C_CA + 2 * D_CONF
C_XQ = C_CZ + D_CONF
C_XZ = C_XQ + D_XATTN

LANES = 128
N1 = 128
NFFT = N1 * N1
NI = SEQ // N1
K1 = N1 // 2 + 1
K1P = 72
KPAIRS = (K1 + 1) // 2
SROWS = 2 * K1P
PITCH = 148
PROJ_COLS = 256
EDGE_UNROLL = 32
MID_UNROLL = 11
TILE = 1024
NTILE = SEQ // TILE
IL = TILE // N1
NSLAB = D_HYENA // LANES
HALO = 16
EXT = TILE + 2 * HALO
VMEM_LIMIT = 60 * 1024 * 1024


def _split(x):
    hi = x.astype(bf16)
    lo = (x - hi.astype(f32)).astype(bf16)
    return hi, lo


def _rep(shape):
    return pl.BlockSpec(shape, lambda *_: (0,) * len(shape), pipeline_mode=pl.Buffered(1))


def _dft_tables():
    k1 = jnp.arange(K1P, dtype=jnp.int32)
    i = jnp.arange(NI, dtype=jnp.int32)
    j = jnp.arange(N1, dtype=jnp.int32)
    valid = (k1 < K1).astype(f32)
    scale = 2.0 * math.pi / NFFT
    ph = (j[:, None, None] * k1[None, :, None] + N1 * k1[None, :, None] * i[None, None, :]) % NFFT
    ang = ph.astype(f32) * scale
    g = jnp.concatenate([jnp.cos(ang) * valid[None, :, None], -jnp.sin(ang) * valid[None, :, None]], axis=1)
    g2 = jnp.concatenate([g[:N1 // 2], g[N1 // 2:]], axis=2).astype(bf16)
    cw = jnp.where((k1 == 0) | (k1 == N1 // 2), 1.0, 2.0) * valid / NFFT
    ph4 = (N1 * i[None, :, None] * k1[None, None, :] + j[:, None, None] * k1[None, None, :]) % NFFT
    ang4 = ph4.astype(f32) * scale
    h = jnp.concatenate([jnp.cos(ang4) * cw, -jnp.sin(ang4) * cw], axis=2)
    h1 = jnp.concatenate([h.astype(bf16), jnp.zeros((N1, NI, 2 * N1 - SROWS), bf16)], axis=2)
    th = ((j[:, None] * j[None, :]) % N1).astype(f32) * (2.0 * math.pi / N1)
    c, s = jnp.cos(th), jnp.sin(th)
    fwd = jnp.concatenate([jnp.concatenate([c, s], 1), jnp.concatenate([-s, c], 1)], 0)
    inv = jnp.concatenate([jnp.concatenate([c, -s], 1), jnp.concatenate([s, c], 1)], 0)

    return g2, h1, fwd.astype(bf16), inv.astype(bf16)


def _positional():
    t = jnp.linspace(0.0, 1.0, SEQ, dtype=f32)[:, None]
    n = jnp.arange(SEQ, dtype=f32)[:, None]
    bands = jnp.linspace(1e-4, FILT_BANDS - 1, FILT_BANDS, dtype=f32)[None, :]
    ang = (2.0 * math.pi / SEQ) * bands * n
    z = jnp.concatenate([t, jnp.cos(ang), -jnp.sin(ang)], axis=-1)
    z = jnp.pad(z, ((0, 0), (0, LANES - FILT_EMB)))
    z = z.reshape(NTILE, IL, N1, LANES).transpose(0, 2, 1, 3).reshape(NTILE, TILE, LANES)
    max_decay = math.log(DECAY_TARGET) / FAST_DECAY_PCT
    min_decay = math.log(DECAY_TARGET) / SLOW_DECAY_PCT
    deltas = jnp.abs(jnp.linspace(min_decay, max_decay, D_HYENA, dtype=f32))
    return z, jnp.concatenate([deltas, deltas])[None, :]


def _filter_kernel(z_ref, w1_ref, b1_ref, fr1_ref, w2_ref, b2_ref, fr2_ref, w3_ref, b3_ref, fr3_ref,
                   w4_ref, dl_ref, o_ref):
    hp = lax.Precision.HIGHEST
    z = z_ref[...]
    h = jnp.sin(fr1_ref[...] * (jnp.dot(z, w1_ref[...], precision=hp, preferred_element_type=f32) + b1_ref[...]))
    h = jnp.sin(fr2_ref[...] * (jnp.dot(h, w2_ref[...], precision=hp, preferred_element_type=f32) + b2_ref[...]))
    h = jnp.sin(fr3_ref[...] * (jnp.dot(h, w3_ref[...], precision=hp, preferred_element_type=f32) + b3_ref[...]))
    h_hi, h_lo = _split(h)
    out = jnp.dot(jnp.concatenate([h_hi, h_lo, h_hi], axis=1), w4_ref[...], preferred_element_type=f32)
    half = TILE // 2
    for p in range(2):
        t = z[:, FILT_HIDDEN * p:FILT_HIDDEN * p + 1]
        h = out[:, 2 * D_HYENA * p:2 * D_HYENA * (p + 1)] * jnp.exp(-t * dl_ref[...])
        col = lax.broadcasted_iota(jnp.int32, h.shape, 1)
        o_ref[half * p:half * (p + 1), :] = jnp.where((col >= D_HYENA) & (t <= 0.0), 0.0, h)


def _filter(z, deltas, w1, b1, fr1, w2, b2, fr2, w3, b3, fr3, w4):
    two = jnp.eye(2, dtype=f32)
    w1p = jnp.kron(two, jnp.pad(w1, ((0, FILT_HIDDEN - FILT_EMB), (0, 0))))
    w2p = jnp.kron(two, w2)
    w3p = jnp.kron(two, w3)
    w4_hi, w4_lo = _split(jnp.kron(two, w4))
    w4p = jnp.concatenate([w4_hi, w4_hi, w4_lo], axis=0)
    vec = lambda v: jnp.concatenate([v, v])[None, :]
    z = jnp.concatenate([z[:, :TILE // 2, :FILT_HIDDEN], z[:, TILE // 2:, :FILT_HIDDEN]], axis=-1)
    args = (z, w1p, vec(b1), vec(fr1), w2p, vec(b2), vec(fr2), w3p, vec(b3), vec(fr3), w4p, deltas)
    specs = [pl.BlockSpec((None, TILE // 2, LANES), lambda t: (t, 0, 0))] + [_rep(a.shape) for a in args[1:]]
    return pl.pallas_call(
        _filter_kernel,
        out_shape=jax.ShapeDtypeStruct((NTILE, TILE, 2 * D_HYENA), f32),
        grid=(NTILE,),
        in_specs=specs,
        out_specs=pl.BlockSpec((None, TILE, 2 * D_HYENA), lambda t: (t, 0, 0)),
        name="hyena_filter",
    )(*args)


def _src_rows(j):
    return pl.ds(pl.multiple_of(j * IL, IL), IL)


def _stage1(src_ref, g2_ref, s_ref):
    zero = jnp.zeros((NI, LANES), bf16)

    def body(p, c):
        q = p + N1 // 2
        xa = src_ref[:, _src_rows(p), :].reshape(NI, LANES).astype(bf16)
        xb = src_ref[:, _src_rows(q), :].reshape(NI, LANES).astype(bf16)
        rhs = jnp.concatenate([jnp.concatenate([xa, zero], axis=1), jnp.concatenate([zero, xb], axis=1)], axis=0)
        a = jnp.dot(g2_ref[p], rhs, preferred_element_type=f32)
        s_ref[pl.ds(p * PITCH, SROWS), :] = a[:, :LANES]
        s_ref[pl.ds(q * PITCH, SROWS), :] = a[:, LANES:]
        return c

    lax.fori_loop(0, N1 // 2, body, 0, unroll=EDGE_UNROLL // 2)


def _load_pair(s_ref, kk):
    k = 2 * kk
    parts = [jnp.concatenate([s_ref[pl.ds(off + k, N1, stride=PITCH), :],
                              s_ref[pl.ds(off + k + 1, N1, stride=PITCH), :]], axis=1) for off in (0, K1P)]
    return jnp.concatenate(parts, axis=0)


def _pair_rows(kk):
    return pl.ds(pl.multiple_of(kk * 2 * N1, 2 * N1), 2 * N1)


def _fast_dft(f2_ref, s_ref, kk):
    return jnp.dot(f2_ref[...], _load_pair(s_ref, kk).astype(bf16), preferred_element_type=f32)


def _spectrum_kernel(hf_ref, hb_ref, g2_ref, f2_ref, o_ref, s_ref):
    _stage1(hf_ref, g2_ref, s_ref)

    def fwd(kk, c):
        o_ref[_pair_rows(kk), :] = _fast_dft(f2_ref, s_ref, kk)
        return c

    lax.fori_loop(0, KPAIRS, fwd, 0, unroll=3)
    _stage1(hb_ref, g2_ref, s_ref)

    def bwd(kk, c):
        a = _fast_dft(f2_ref, s_ref, kk)
        r0 = pl.multiple_of(kk * 2 * N1, 2 * N1)
        o_ref[pl.ds(r0, N1), :] = o_ref[pl.ds(r0, N1), :] + a[:N1]
        o_ref[pl.ds(r0 + N1, N1), :] = o_ref[pl.ds(r0 + N1, N1), :] - a[N1:]
        return c

    lax.fori_loop(0, KPAIRS, bwd, 0, unroll=3)


def _spectrum(h, g2, f2):
    return pl.pallas_call(
        _spectrum_kernel,
        out_shape=jax.ShapeDtypeStruct((NSLAB, KPAIRS * 2 * N1, 2 * LANES), f32),
        grid=(NSLAB,),
        in_specs=[
            pl.BlockSpec((NTILE, TILE, LANES), lambda c: (0, 0, c)),
            pl.BlockSpec((NTILE, TILE, LANES), lambda c: (0, 0, NSLAB + c)),
            _rep(g2.shape),
            _rep(f2.shape),
        ],
        out_specs=pl.BlockSpec((None, KPAIRS * 2 * N1, 2 * LANES), lambda c: (c, 0, 0)),
        scratch_shapes=[pltpu.VMEM((N1 * PITCH, LANES), f32)],
        compiler_params=pltpu.CompilerParams(vmem_limit_bytes=VMEM_LIMIT),
        name="hyena_spectrum",
    )(h, h, g2, f2)


def _conv_kernel(w_ref, kf_ref, skip_ref, g2_ref, h1_ref, f2_ref, f2i_ref, o_ref, s_ref):
    _stage1(w_ref, g2_ref, s_ref)

    def mid(kk, c):
        a = _fast_dft(f2_ref, s_ref, kk)
        kf = kf_ref[_pair_rows(kk), :]
        xr, xi, kr, ki = a[:N1], a[N1:], kf[:N1], kf[N1:]
        y = jnp.concatenate([xr * kr - xi * ki, xr * ki + xi * kr], axis=0)
        b = jnp.dot(f2i_ref[...], y.astype(bf16), preferred_element_type=f32)
        k = 2 * kk
        for p in range(2):
            s_ref[pl.ds(k + p, N1, stride=PITCH), :] = b[:N1, LANES * p:LANES * (p + 1)]
            s_ref[pl.ds(K1P + k + p, N1, stride=PITCH), :] = b[N1:, LANES * p:LANES * (p + 1)]
        return c

    lax.fori_loop(0, KPAIRS, mid, 0, unroll=MID_UNROLL)
    pad = jnp.zeros((2 * N1 - SROWS, LANES), bf16)
    skip = skip_ref[...]

    def last(j, c):
        b = s_ref[pl.ds(j * PITCH, SROWS), :].astype(bf16)
        y = jnp.dot(h1_ref[j], jnp.concatenate([b, pad], axis=0), preferred_element_type=f32)
        rows = _src_rows(j)
        x = w_ref[:, rows, :].reshape(NI, LANES)
        o_ref[:, rows, :] = (y + x * skip).reshape(NTILE, IL, LANES)
        return c

    lax.fori_loop(0, N1, last, 0, unroll=EDGE_UNROLL)


def _long_conv(wp, kf, skip, g2, h1, f2, f2i):
    nb = wp.shape[0]
    blk = pl.BlockSpec((None, NTILE, None, TILE, LANES), lambda c, b: (b, 0, c, 0, 0))
    return pl.pallas_call(
        _conv_kernel,
        out_shape=jax.ShapeDtypeStruct(wp.shape, f32),
        grid=(NSLAB, nb),
        in_specs=[
            blk,
            pl.BlockSpec((None, KPAIRS * 2 * N1, 2 * LANES), lambda c, b: (c, 0, 0), pipeline_mode=pl.Buffered(1)),
            pl.BlockSpec((None, 1, LANES), lambda c, b: (c, 0, 0)),
            _rep(g2.shape),
            _rep(h1.shape),
            _rep(f2.shape),
            _rep(f2i.shape),
        ],
        out_specs=blk,
        scratch_shapes=[pltpu.VMEM((N1 * PITCH, LANES), f32)],
        compiler_params=pltpu.CompilerParams(vmem_limit_bytes=VMEM_LIMIT),
        name="hyena_long_conv",
    )(wp, kf, skip, g2, h1, f2, f2i)


def _rms(v, g):
    return v * lax.rsqrt(jnp.mean(v * v, axis=-1, keepdims=True) + EPS) * g


def _kv_kernel(mem_ref, g_ref, wkt_ref, wv_ref, kbd_ref, vbd_ref):
    mn = _rms(mem_ref[...], g_ref[...]).astype(bf16)
    kt = lax.dot_general(wkt_ref[...], mn, (((1,), (1,)), ((), ())), preferred_element_type=f32)
    kt = kt * (HEAD_DIM ** -0.5)
    v = jnp.dot(mn, wv_ref[...], preferred_element_type=f32)
    kb = jnp.concatenate([kt] * N_XHEADS, axis=1)
    hd_bits, mem_bits = HEAD_DIM.bit_length() - 1, N_MEM.bit_length() - 1
    r = lax.broadcasted_iota(jnp.int32, kb.shape, 0) >> hd_bits
    c = lax.broadcasted_iota(jnp.int32, kb.shape, 1) >> mem_bits
    kbd_ref[...] = jnp.where(r == c, kb, 0.0).astype(bf16)
    vb = jnp.concatenate([v] * N_XHEADS, axis=0)
    r = lax.broadcasted_iota(jnp.int32, vb.shape, 0) >> mem_bits
    c = lax.broadcasted_iota(jnp.int32, vb.shape, 1) >> hd_bits
    vbd_ref[...] = jnp.where(r == c, vb, 0.0).astype(bf16)


def _memory_kv(mem, g, wkt, wv):
    nb = mem.shape[0]
    return pl.pallas_call(
        _kv_kernel,
        out_shape=(jax.ShapeDtypeStruct((nb, D_XATTN, N_XHEADS * N_MEM), bf16),
                   jax.ShapeDtypeStruct((nb, N_XHEADS * N_MEM, D_XATTN), bf16)),
        grid=(nb,),
        in_specs=[pl.BlockSpec((None, N_MEM, D_MODEL), lambda b: (b, 0, 0)), _rep(g.shape), _rep(wkt.shape),
                  _rep(wv.shape)],
        out_specs=(pl.BlockSpec((None, D_XATTN, N_XHEADS * N_MEM), lambda b: (b, 0, 0)),
                   pl.BlockSpec((None, N_XHEADS * N_MEM, D_XATTN), lambda b: (b, 0, 0))),
        name="memory_kv",
    )(mem, g, wkt, wv)


def _silu(v):
    return v * jax.nn.sigmoid(v)


def _mix_kernel(x_ref, xp_ref, xn_ref, ng_ref, win_ref, sw_ref, sb_ref, dw_ref, db_ref, lng_ref, lnb_ref,
                kbd_ref, vbd_ref, w_ref, g_ref, y_ref, hext, pbuf, glu):
    t = pl.program_id(1)
    ng = ng_ref[...]
    hext[0:HALO, :] = _rms(xp_ref[...] * (t > 0).astype(f32), ng).astype(bf16)
    hext[HALO + TILE:EXT, :] = _rms(xn_ref[...] * (t < NTILE - 1).astype(f32), ng).astype(bf16)

    def project(b0, b1):
        m0, m1 = max(b0, HALO), min(b1, HALO + TILE)
        hext[m0:m1, :] = _rms(x_ref[m0 - HALO:m1 - HALO, :], ng).astype(bf16)
        order = sorted(range(0, D_IN, PROJ_COLS), key=lambda c: (not C_CA <= c < C_CZ, not C_XQ <= c < C_XZ, c))
        for c0 in order:
            width = min(PROJ_COLS, D_IN - c0)
            res = jnp.dot(hext[b0:b1, :], win_ref[:, c0:c0 + width], preferred_element_type=f32)
            for i in range(width // LANES):
                pbuf[c0 // LANES + i, b0:b1, :] = res[:, LANES * i:LANES * (i + 1)]
        for c in range(D_CONF // LANES):
            glu[c, b0:b1, :] = (pbuf[C_CA // LANES + c, b0:b1, :]
                                * jax.nn.sigmoid(pbuf[(C_CA + D_CONF) // LANES + c, b0:b1, :]))

    def cols(c0, width, rows):
        return jnp.concatenate([pbuf[c0 // LANES + i, rows, :] for i in range(width // LANES)], axis=1)

    kbd = kbd_ref[...]
    vbd = vbd_ref[...]
    bounds = [0] + [2 * HALO + 2 * N1 * b for b in range(1, IL // 2)] + [EXT]
    done = 0
    for r in range(IL):
        while bounds[done] < N1 * r + N1 + 2 * HALO:
            project(bounds[done], bounds[done + 1])
            done += 1
        e0 = HALO + N1 * r
        rows = slice(e0, e0 + N1)
        orow = slice(N1 * r, N1 * (r + 1))

        def sconv(slab):
            lanes = slice(LANES * slab, LANES * (slab + 1))
            acc = sb_ref[:, lanes]
            for k in range(SHORT_K):
                acc = acc + pbuf[slab, e0 - 1 + k:e0 - 1 + k + N1, :] * sw_ref[k:k + 1, lanes]
            return acc

        for s in range(NSLAB):
            wv = sconv(2 * NSLAB + s) * sconv(NSLAB + s)
            w_ref[s, pl.ds(r, N1, stride=IL), :] = wv
            gate = sconv(s) * _silu(pbuf[C_HZ // LANES + s, rows, :])
            g_ref[orow, LANES * s:LANES * (s + 1)] = gate.astype(bf16)

        accs = []
        for c in range(D_CONF // LANES):
            lanes = slice(LANES * c, LANES * (c + 1))
            acc = db_ref[:, lanes]
            for k in range(CONF_K):
                lo = e0 - CONF_K // 2 + k
                acc = acc + glu[c, lo:lo + N1, :] * dw_ref[k:k + 1, lanes]
            accs.append(acc)
        acc = jnp.concatenate(accs, axis=1)
        mu = jnp.mean(acc, axis=-1, keepdims=True)
        xc = acc - mu
        ln = xc * lax.rsqrt(jnp.mean(xc * xc, axis=-1, keepdims=True) + EPS) * lng_ref[...] + lnb_ref[...]
        y_ref[orow, 0:D_CONF] = (_silu(ln) * _silu(cols(C_CZ, D_CONF, rows))).astype(bf16)

        q = cols(C_XQ, D_XATTN, rows).astype(bf16)
        s_all = jnp.dot(q, kbd, preferred_element_type=f32)
        ps = []
        for h in range(N_XHEADS):
            sh = s_all[:, N_MEM * h:N_MEM * (h + 1)]
            e = jnp.exp(sh - jnp.max(sh, axis=-1, keepdims=True))
            ps.append((e / jnp.sum(e, axis=-1, keepdims=True)).astype(bf16))
        o = jnp.dot(jnp.concatenate(ps, axis=1), vbd, preferred_element_type=f32)
        y_ref[orow, D_CONF:D_CONF + D_XATTN] = (o * _silu(cols(C_XZ, D_XATTN, rows))).astype(bf16)


def _mix(x, ng, win, sw, sb, dw, db, lng, lnb, kbd, vbd):
    nb = x.shape[0]
    hb = TILE // HALO
    nhb = SEQ // HALO
    small = [ng, win, sw, sb, dw, db, lng, lnb]
    return pl.pallas_call(
        _mix_kernel,
        out_shape=(jax.ShapeDtypeStruct((nb, NTILE, NSLAB, TILE, LANES), f32),
                   jax.ShapeDtypeStruct((nb, SEQ, D_HYENA), bf16),
                   jax.ShapeDtypeStruct((nb, SEQ, D_CONF + D_XATTN), bf16)),
        grid=(nb, NTILE),
        in_specs=[
            pl.BlockSpec((None, TILE, D_MODEL), lambda b, t: (b, t, 0)),
            pl.BlockSpec((None, HALO, D_MODEL), lambda b, t: (b, jnp.maximum(t * hb - 1, 0), 0)),
            pl.BlockSpec((None, HALO, D_MODEL), lambda b, t: (b, jnp.minimum((t + 1) * hb, nhb - 1), 0)),
        ] + [_rep(a.shape) for a in small] + [
            pl.BlockSpec((None, D_XATTN, N_XHEADS * N_MEM), lambda b, t: (b, 0, 0)),
            pl.BlockSpec((None, N_XHEADS * N_MEM, D_XATTN), lambda b, t: (b, 0, 0)),
        ],
        out_specs=(pl.BlockSpec((None, None, NSLAB, TILE, LANES), lambda b, t: (b, t, 0, 0, 0)),
                   pl.BlockSpec((None, TILE, D_HYENA), lambda b, t: (b, t, 0)),
                   pl.BlockSpec((None, TILE, D_CONF + D_XATTN), lambda b, t: (b, t, 0))),
        scratch_shapes=[pltpu.VMEM((EXT, D_MODEL), bf16), pltpu.VMEM((D_IN // LANES, EXT, LANES), f32),
                        pltpu.VMEM((D_CONF // LANES, EXT, LANES), f32)],
        compiler_params=pltpu.CompilerParams(vmem_limit_bytes=VMEM_LIMIT),
        name="token_mix",
    )(x, x, x, *small, kbd, vbd)


def _out_kernel(x_ref, g_ref, c_ref, y_ref, wout_ref, fg_ref, o_ref, mix, *, final):
    for r in range(IL):
        rows = slice(N1 * r, N1 * (r + 1))
        for s in range(NSLAB):
            cols = slice(LANES * s, LANES * (s + 1))
            conv = c_ref[s, pl.ds(r, N1, stride=IL), :]
            mix[rows, cols] = (g_ref[rows, cols].astype(f32) * conv).astype(bf16)
    mix[:, D_HYENA:] = y_ref[...]
    xn = x_ref[...] + jnp.dot(mix[...], wout_ref[...], preferred_element_type=f32)
    o_ref[...] = _rms(xn, fg_ref[...]) if final else xn


def _out_proj(x, g, cp, y, wout, fg, final):
    nb = x.shape[0]
    return pl.pallas_call(
        functools.partial(_out_kernel, final=final),
        out_shape=jax.ShapeDtypeStruct(x.shape, f32),
        grid=(nb, NTILE),
        in_specs=[
            pl.BlockSpec((None, TILE, D_MODEL), lambda b, t: (b, t, 0)),
            pl.BlockSpec((None, TILE, D_HYENA), lambda b, t: (b, t, 0)),
            pl.BlockSpec((None, None, NSLAB, TILE, LANES), lambda b, t: (b, t, 0, 0, 0)),
            pl.BlockSpec((None, TILE, D_CONF + D_XATTN), lambda b, t: (b, t, 0)),
            _rep(wout.shape),
            _rep(fg.shape),
        ],
        out_specs=pl.BlockSpec((None, TILE, D_MODEL), lambda b, t: (b, t, 0)),
        scratch_shapes=[pltpu.VMEM((TILE, D_MODEL), bf16)],
        compiler_params=pltpu.CompilerParams(vmem_limit_bytes=VMEM_LIMIT),
        name="out_proj",
    )(x, g, cp, y, wout, fg)


def kernel(x_prompt, x_sample, mem_prompt, mem_sample, norm_g, mem_norm_g, w_in, hy_short_w, hy_short_b, hy_f_w1, hy_f_b1, hy_f_fr1, hy_f_w2, hy_f_b2, hy_f_fr2, hy_f_w3, hy_f_b3, hy_f_fr3, hy_f_w4, hy_skip, cf_dw_w, cf_dw_b, cf_ln_g, cf_ln_b, xa_w_kv, w_out, final_g):
    g2, h1, f2, f2i = _dft_tables()
    z, deltas = _positional()
    row = lambda v: v[None, :]
    xs = [x_prompt, x_sample]
    mems = [mem_prompt, mem_sample]
    for l in range(DEPTH):
        h = _filter(z, deltas, hy_f_w1[l], hy_f_b1[l], hy_f_fr1[l], hy_f_w2[l], hy_f_b2[l], hy_f_fr2[l],
                    hy_f_w3[l], hy_f_b3[l], hy_f_fr3[l], hy_f_w4[l])
        kf = _spectrum(h, g2, f2)
        win, wout = w_in[l].astype(bf16), w_out[l].astype(bf16)
        wkt, wv = xa_w_kv[l][:, :D_XATTN].T.astype(bf16), xa_w_kv[l][:, D_XATTN:].astype(bf16)
        skip = hy_skip[l].reshape(NSLAB, 1, LANES)
        for i in range(len(xs)):
            kbd, vbd = _memory_kv(mems[i], row(mem_norm_g[l]), wkt, wv)
            wp, g, y = _mix(xs[i], row(norm_g[l]), win, hy_short_w[l], row(hy_short_b[l]), cf_dw_w[l],
                            row(cf_dw_b[l]), row(cf_ln_g[l]), row(cf_ln_b[l]), kbd, vbd)
            cp = _long_conv(wp, kf, skip, g2, h1, f2, f2i)
            xs[i] = _out_proj(xs[i], g, cp, y, wout, row(final_g), final=(l == DEPTH - 1))
    return xs[0], xs[1]
```
